```python
import math
import jax
import jax.numpy as jnp
from jax import lax
import numpy as np

D_MODEL = 1024
BATCH = 2
SEQ = 8192
DEPTH = 2

CTX_LEN = 256
GRID_W = 64
D_MIX = D_MODEL
MLA_HEADS = 4
MLA_W = D_MIX // 2
MLA_DV = MLA_W // MLA_HEADS
MLA_NOPE = 128
MLA_ROPE = 64
Q_LORA = 384
KV_LORA = 256
DIFF_HEADS = 4
DIFF_W = D_MIX // 4
DIFF_DV = DIFF_W // DIFF_HEADS
DIFF_DQK = DIFF_DV // 2
CHUNK_W = D_MIX - MLA_W - DIFF_W
CHUNK_GROUPS = 4
CHUNK_GW = CHUNK_W // CHUNK_GROUPS
CHUNK = 128
D_FF = 2816
CONV_W = 3
Q_BLOCK = 128
ROPE_BASE = 10000.0
EPS = 1e-6
DN_ALPHA = (2 * DEPTH) ** 0.25
DN_BETA = (8 * DEPTH) ** -0.25
MLA_SCALE = (MLA_NOPE + MLA_ROPE) ** -0.5
DIFF_SCALE = DIFF_DQK ** -0.5
IN_SPLITS = [Q_LORA,
             Q_LORA + KV_LORA,
             Q_LORA + KV_LORA + MLA_ROPE,
             Q_LORA + KV_LORA + MLA_ROPE + DIFF_W,
             Q_LORA + KV_LORA + MLA_ROPE + 2 * DIFF_W,
             Q_LORA + KV_LORA + MLA_ROPE + 3 * DIFF_W]
IN_W = Q_LORA + KV_LORA + MLA_ROPE + 3 * DIFF_W + 2 * CHUNK_W

kernel_name = "hymba_mla_diff_chunkmlp_convffn_deepnorm_dit"

PARAM_NAMES = ["ada_w", "ada_b", "w_in", "mla_gq", "mla_wuq", "mla_gkv", "mla_wukv",
               "diff_lq1", "diff_lk1", "diff_lq2", "diff_lk2", "diff_subln_g",
               "sgu_ln_g", "sgu_ln_b", "sgu_ws", "sgu_bs", "w_o", "ln1_g", "ln1_b",
               "ffn_wup", "ffn_convw", "ffn_convb", "ffn_wdown", "ln2_g", "ln2_b"]


def layer_norm(x, g, b):
    xf = x.astype(jnp.float32)
    mu = jnp.mean(xf, -1, keepdims=True)
    var = jnp.mean(jnp.square(xf - mu), -1, keepdims=True)
    return ((xf - mu) * lax.rsqrt(var + EPS)).astype(x.dtype) * g + b


def rms_norm(x, g):
    xf = x.astype(jnp.float32)
    return (xf * lax.rsqrt(jnp.mean(xf * xf, -1, keepdims=True) + EPS)).astype(x.dtype) * g


def rope_1d(x, pos):
    n = x.shape[-1] // 2
    inv = ROPE_BASE ** (-jnp.arange(n, dtype=jnp.float32) / n)
    ang = pos.astype(jnp.float32)[:, None] * inv[None, :]
    cos = jnp.cos(ang)[None, :, None, :].astype(x.dtype)
    sin = jnp.sin(ang)[None, :, None, :].astype(x.dtype)
    x1, x2 = x[..., :n], x[..., n:]
    return jnp.concatenate([x1 * cos - x2 * sin, x2 * cos + x1 * sin], -1)


def axial_rope(x, rows, cols):
    half = x.shape[-1] // 2
    return jnp.concatenate([rope_1d(x[..., :half], rows), rope_1d(x[..., half:], cols)], -1)


def attention(q, k, v, scale):
    B, Sq, H, Dk = q.shape
    Dv = v.shape[-1]
    nblk = Sq // Q_BLOCK
    qb = jnp.moveaxis(q.reshape(B, nblk, Q_BLOCK, H, Dk), 1, 0)

    def one_block(qblk):
        s = jnp.einsum('bqhd,bkhd->bhqk', qblk, k).astype(jnp.float32) * scale
        p = jax.nn.softmax(s, axis=-1).astype(v.dtype)
        return jnp.einsum('bhqk,bkhd->bqhd', p, v)

    o = lax.map(one_block, qb)
    return jnp.moveaxis(o, 0, 1).reshape(B, Sq, H, Dv)


def project_stream(z, p, rows, cols):
    B, S, _ = z.shape
    if rows is None:
        rot = lambda t: t
    else:
        rot = lambda t: axial_rope(t, rows, cols)
    q_lat, kv_lat, k_r, dq, dk, dv, zch = jnp.split(z, IN_SPLITS, axis=-1)
    qm = (rms_norm(q_lat, p["mla_gq"]) @ p["mla_wuq"]).reshape(B, S, MLA_HEADS, MLA_NOPE + MLA_ROPE)
    qm = jnp.concatenate([qm[..., :MLA_NOPE], rot(qm[..., MLA_NOPE:])], -1)
    kvm = (rms_norm(kv_lat, p["mla_gkv"]) @ p["mla_wukv"]).reshape(B, S, MLA_HEADS, MLA_NOPE + MLA_DV)
    kr = rot(k_r[:, :, None, :])
    km = jnp.concatenate([kvm[..., :MLA_NOPE], jnp.broadcast_to(kr, (B, S, MLA_HEADS, MLA_ROPE))], -1)
    vm = kvm[..., MLA_NOPE:]

    def two_maps(t):
        t = t.reshape(B, S, DIFF_HEADS, 2, DIFF_DQK)
        return rot(jnp.swapaxes(t, 2, 3).reshape(B, S, 2 * DIFF_HEADS, DIFF_DQK))

    qd = two_maps(dq)
    kd = two_maps(dk)
    vd = jnp.tile(dv.reshape(B, S, DIFF_HEADS, DIFF_DV), (1, 1, 2, 1))
    return qm, km, vm, qd, kd, vd, zch


def chunk_mix(z, p):
    z = jax.nn.gelu(z)
    u, v = jnp.split(z, 2, axis=-1)
    v = layer_norm(v, p["sgu_ln_g"], p["sgu_ln_b"])
    B, S, _ = v.shape
    vg = v.reshape(B, S // CHUNK, CHUNK, CHUNK_GROUPS, CHUNK_GW)
    mixed = jnp.einsum('gpq,bnqgc->bnpgc', p["sgu_ws"], vg) + p["sgu_bs"].T[None, None, :, :, None]
    return u * mixed.reshape(B, S, CHUNK_W)


def mixer(h, hc, rows, cols, p, layer_idx, need_ctx):
    qm, km, vm, qd, kd, vd, zch = project_stream(h @ p["w_in"], p, rows, cols)
    qmc, kmc, vmc, qdc, kdc, vdc, zchc = project_stream(hc @ p["w_in"], p, None, None)
    lam_init = 0.8 - 0.6 * math.exp(-0.3 * layer_idx)
    lam = (jnp.exp(jnp.sum(p["diff_lq1"] * p["diff_lk1"]))
           - jnp.exp(jnp.sum(p["diff_lq2"] * p["diff_lk2"])) + lam_init)

    def merge(o_m, o_d, z_c):
        Bn, Sn = o_m.shape[:2]
        d = o_d[:, :, :DIFF_HEADS] - lam * o_d[:, :, DIFF_HEADS:]
        d = rms_norm(d, p["diff_subln_g"]) * (1.0 - lam_init)
        y = jnp.concatenate([o_m.reshape(Bn, Sn, MLA_W), d.reshape(Bn, Sn, DIFF_W), chunk_mix(z_c, p)], -1)
        return y @ p["w_o"]

    o_m = attention(qm, jnp.concatenate([kmc, km], 1), jnp.concatenate([vmc, vm], 1), MLA_SCALE)
    o_d = attention(qd, jnp.concatenate([kdc, kd], 1), jnp.concatenate([vdc, vd], 1), DIFF_SCALE)
    y = merge(o_m, o_d, zch)
    yc = None
    if need_ctx:
        yc = merge(attention(qmc, kmc, vmc, MLA_SCALE), attention(qdc, kdc, vdc, DIFF_SCALE), zchc)
    return y, yc


def dwconv3(h, w, b):
    hp = jnp.pad(h, ((0, 0), (1, 1), (0, 0)))
    return hp[:, :-2] * w[0] + hp[:, 1:-1] * w[1] + hp[:, 2:] * w[2] + b


def conv_ffn(h, p):
    u = dwconv3(h @ p["ffn_wup"], p["ffn_convw"], p["ffn_convb"])
    gate, val = jnp.split(u, 2, axis=-1)
    return (jax.nn.silu(gate) * val) @ p["ffn_wdown"]


def layer(x, xc, mod, modc, rows, cols, p, layer_idx, need_ctx):
    sh1, sc1, g1, sh2, sc2, g2 = jnp.split(mod, 6, axis=-1)
    csh1, csc1, cg1, csh2, csc2, cg2 = jnp.split(modc, 6, axis=-1)
    y, yc = mixer(x * (1.0 + sc1) + sh1, xc * (1.0 + csc1) + csh1, rows, cols, p, layer_idx, need_ctx)
    x = layer_norm(DN_ALPHA * x + g1 * y, p["ln1_g"], p["ln1_b"])
    x = layer_norm(DN_ALPHA * x + g2 * conv_ffn(x * (1.0 + sc2) + sh2, p), p["ln2_g"], p["ln2_b"])
    if need_ctx:
        xc = layer_norm(DN_ALPHA * xc + cg1 * yc, p["ln1_g"], p["ln1_b"])
        xc = layer_norm(DN_ALPHA * xc + cg2 * conv_ffn(xc * (1.0 + csc2) + csh2, p), p["ln2_g"], p["ln2_b"])
    return x, xc


def setup_inputs(seed: int = 0) -> dict:
    key = jax.random.key(seed)
    ks = iter(jax.random.split(key, 40))
    f32 = jnp.float32
    nrm = lambda shape, s: jax.random.normal(next(ks), shape, f32) * s
    gain = lambda shape: 1.0 + nrm(shape, 0.02)
    L = DEPTH
    return {
        "x": nrm((BATCH, SEQ, D_MODEL), 1.0),
        "c": nrm((BATCH, D_MODEL), 1.0),
        "ctx": nrm((BATCH, CTX_LEN, D_MODEL), 1.0),
        "c_ctx": nrm((D_MODEL,), 1.0),
        "ada_w": nrm((L, D_MODEL, 6 * D_MODEL), 0.5 * D_MODEL ** -0.5),
        "ada_b": nrm((L, 6 * D_MODEL), 0.02),
        "w_in": nrm((L, D_MODEL, IN_W), D_MODEL ** -0.5),
        "mla_gq": gain((L, Q_LORA)),
        "mla_wuq": nrm((L, Q_LORA, MLA_HEADS * (MLA_NOPE + MLA_ROPE)), Q_LORA ** -0.5),
        "mla_gkv": gain((L, KV_LORA)),
        "mla_wukv": nrm((L, KV_LORA, MLA_HEADS * (MLA_NOPE + MLA_DV)), KV_LORA ** -0.5),
        "diff_lq1": nrm((L, DIFF_DQK), 0.1),
        "diff_lk1": nrm((L, DIFF_DQK), 0.1),
        "diff_lq2": nrm((L, DIFF_DQK), 0.1),
        "diff_lk2": nrm((L, DIFF_DQK), 0.1),
        "diff_subln_g": gain((L, DIFF_DV)),
        "sgu_ln_g": gain((L, CHUNK_W)),
        "sgu_ln_b": nrm((L, CHUNK_W), 0.02),
        "sgu_ws": nrm((L, CHUNK_GROUPS, CHUNK, CHUNK), CHUNK ** -0.5),
        "sgu_bs": gain((L, CHUNK_GROUPS, CHUNK)),
        "w_o": nrm((L, D_MIX, D_MODEL), DN_BETA * D_MIX ** -0.5),
        "ln1_g": gain((L, D_MODEL)),
        "ln1_b": nrm((L, D_MODEL), 0.02),
        "ffn_wup": nrm((L, D_MODEL, 2 * D_FF), D_MODEL ** -0.5),
        "ffn_convw": nrm((L, CONV_W, 2 * D_FF), CONV_W ** -0.5),
        "ffn_convb": nrm((L, 2 * D_FF), 0.01),
        "ffn_wdown": nrm((L, D_FF, D_MODEL), DN_BETA * D_FF ** -0.5),
        "ln2_g": gain((L, D_MODEL)),
        "ln2_b": nrm((L, D_MODEL), 0.02),
    }


def reference(x, c, ctx, c_ctx, ada_w, ada_b, w_in, mla_gq, mla_wuq, mla_gkv, mla_wukv,
              diff_lq1, diff_lk1, diff_lq2, diff_lk2, diff_subln_g, sgu_ln_g, sgu_ln_b, sgu_ws, sgu_bs,
              w_o, ln1_g, ln1_b, ffn_wup, ffn_convw, ffn_convb, ffn_wdown, ln2_g, ln2_b):
    S = x.shape[1]
    ROWS = S // GRID_W
    rows = jnp.repeat(jnp.arange(ROWS, dtype=jnp.int32), GRID_W)
    cols = jnp.tile(jnp.arange(GRID_W, dtype=jnp.int32), ROWS)
    stacked = [ada_w, ada_b, w_in, mla_gq, mla_wuq, mla_gkv, mla_wukv,
               diff_lq1, diff_lk1, diff_lq2, diff_lk2, diff_subln_g,
               sgu_ln_g, sgu_ln_b, sgu_ws, sgu_bs, w_o, ln1_g, ln1_b,
               ffn_wup, ffn_convw, ffn_convb, ffn_wdown, ln2_g, ln2_b]
    s_c = jax.nn.silu(c)
    s_cc = jax.nn.silu(c_ctx)
    xc = ctx
    for l in range(DEPTH):
        p = {name: arr[l] for name, arr in zip(PARAM_NAMES, stacked)}
        mod = (s_c @ p["ada_w"] + p["ada_b"])[:, None, :]
        modc = (s_cc @ p["ada_w"] + p["ada_b"])[None, None, :]
        x, xc = layer(x, xc, mod, modc, rows, cols, p, l, l < DEPTH - 1)
    return x
```

```python
import functools
import math

import jax
import jax.numpy as jnp
from jax import lax
from jax.experimental import pallas as pl
from jax.experimental.pallas import tpu as pltpu

F32 = jnp.float32
BF16 = jnp.bfloat16

D = 1024
S = 8192
C = 256
GRID_W = 64
DEPTH = 2
T = 512
ST = S + 2 * C
NT = ST // T
NT_LAT = S // T
SK = S + C
HEADS = 4
Q_LORA = 384
KV_LORA = 256
NOPE = 128
ROPE = 64
DV = 128
DIFF_W = 256
DIFF_DQK = 32
DIFF_DV = 64
CHUNK_W = 256
CHUNK = 128
GROUPS = 4
D_FF = 2816
FF_CHUNK = 256
HALO = 16
TK = 512
EPS = 1e-6
ROPE_BASE = 10000.0
DN_ALPHA = (2 * DEPTH) ** 0.25
MLA_SCALE = (NOPE + ROPE) ** -0.5
DIFF_SCALE = DIFF_DQK ** -0.5
ZW = 2688
MOD_ROWS = 8
CTX_ROW = 2
VMEM_LIMIT = 56 * 1024 * 1024

_NT_DIMS = (((1,), (1,)), ((), ()))


def _dot(a, b):
    return jnp.dot(a, b, preferred_element_type=F32)


def _rms(x, g):
    return (x * lax.rsqrt(jnp.mean(x * x, axis=-1, keepdims=True) + EPS)) * g


def _ln(x, g, b):
    mu = jnp.mean(x, axis=-1, keepdims=True)
    xc = x - mu
    var = jnp.mean(xc * xc, axis=-1, keepdims=True)
    return (xc * lax.rsqrt(var + EPS)) * g + b


def _params(n_axes):
    return pltpu.CompilerParams(dimension_semantics=("arbitrary",) * n_axes,
                                vmem_limit_bytes=VMEM_LIMIT)


def _const_spec(shape):
    nd = len(shape)
    return pl.BlockSpec(shape, lambda *_: (0,) * nd, pipeline_mode=pl.Buffered(1))


MOD_BLOCK = 1536


def _mod_kernel(c_ref, w_ref, b_ref, o_ref):
    c = c_ref[...]
    sc = (c * (1.0 / (1.0 + jnp.exp(-c)))).astype(BF16)
    o_ref[...] = _dot(sc, w_ref[...].astype(BF16)) + b_ref[...]


def _modulation(cvec, ada_w, ada_b):
    return pl.pallas_call(
        _mod_kernel,
        out_shape=jax.ShapeDtypeStruct((DEPTH, MOD_ROWS, 6 * D), F32),
        grid=(DEPTH, 6 * D // MOD_BLOCK),
        in_specs=[pl.BlockSpec((MOD_ROWS, D), lambda l, j: (0, 0)),
                  pl.BlockSpec((None, D, MOD_BLOCK), lambda l, j: (l, 0, j)),
                  pl.BlockSpec((None, 1, MOD_BLOCK), lambda l, j: (l, 0, j))],
        out_specs=pl.BlockSpec((None, MOD_ROWS, MOD_BLOCK), lambda l, j: (l, 0, j)),
        compiler_params=_params(2),
        name="modulation",
    )(cvec, ada_w, ada_b.reshape(DEPTH, 1, 6 * D))


def _mod_row(mod_ref, tile):
    return jnp.where(tile == NT_LAT, CTX_ROW, pl.program_id(0))


def _inproj_kernel(x_ref, mod_ref, win_ref, gq_ref, wuq_ref, wuqc_ref, gkv_ref, wk_ref, wv_ref,
                   ckr_ref, skr_ref, cd_ref, sd_ref, lng_ref, lnb_ref, ws_ref, bs_ref,
                   qm_ref, km_ref, vm_ref, qd_ref, kd_ref, vd_ref, cm_ref):
    r = _mod_row(mod_ref, pl.program_id(1))
    sh1 = mod_ref[pl.ds(r, 1), 0:D]
    sc1 = mod_ref[pl.ds(r, 1), D:2 * D]
    h = (x_ref[...] * (1.0 + sc1) + sh1).astype(BF16)
    z = _dot(h, win_ref[...])

    ckr = ckr_ref[...]
    skr = skr_ref[...]
    cd = cd_ref[...]
    sd = sd_ref[...]

    qn = _rms(z[:, 0:Q_LORA], gq_ref[...]).astype(BF16)
    qa = _dot(qn, wuq_ref[...])
    qc = _dot(qn, wuqc_ref[...])
    for hd in range(HEADS):
        nope = qa[:, hd * 256:hd * 256 + NOPE]
        rot = qa[:, hd * 256 + NOPE:(hd + 1) * 256] * ckr + qc[:, hd * 128:(hd + 1) * 128] * skr
        qm_ref[:, hd * 256:hd * 256 + NOPE] = (nope * MLA_SCALE).astype(BF16)
        qm_ref[:, hd * 256 + NOPE:(hd + 1) * 256] = (rot * MLA_SCALE).astype(BF16)

    kvn = _rms(z[:, Q_LORA:Q_LORA + KV_LORA], gkv_ref[...]).astype(BF16)
    kn = _dot(kvn, wk_ref[...])
    vm_ref[...] = _dot(kvn, wv_ref[...]).astype(BF16)
    kr = (z[:, 640:768] * ckr + z[:, 2048:2176] * skr).astype(BF16)
    for hd in range(HEADS):
        km_ref[:, hd * 256:hd * 256 + NOPE] = kn[:, hd * 128:(hd + 1) * 128].astype(BF16)
        km_ref[:, hd * 256 + NOPE:(hd + 1) * 256] = kr

    for blk in range(2):
        lo = blk * 128
        qd_ref[:, lo:lo + 128] = ((z[:, 768 + lo:896 + lo] * cd + z[:, 2176 + lo:2304 + lo] * sd)
                                  * DIFF_SCALE).astype(BF16)
        kd_ref[:, lo:lo + 128] = (z[:, 1024 + lo:1152 + lo] * cd
                                  + z[:, 2432 + lo:2560 + lo] * sd).astype(BF16)
    vd_ref[...] = z[:, 1280:1536].astype(BF16)

    zc = z[:, 1536:2048]
    zc = 0.5 * zc * (1.0 + jnp.tanh(math.sqrt(2.0 / math.pi) * (zc + 0.044715 * (zc * zc * zc))))
    u = zc[:, :CHUNK_W]
    v = _ln(zc[:, CHUNK_W:], lng_ref[...], lnb_ref[...]).astype(BF16)
    group = lax.broadcasted_iota(jnp.int32, (CHUNK, CHUNK_W), 1) // (CHUNK_W // GROUPS)
    ws = ws_ref[...]
    for n in range(T // CHUNK):
        mixed_all = _dot(ws, v[n * CHUNK:(n + 1) * CHUNK, :])
        mixed = mixed_all[0:CHUNK]
        for g in range(1, GROUPS):
            mixed = jnp.where(group == g, mixed_all[g * CHUNK:(g + 1) * CHUNK], mixed)
        cm_ref[n * CHUNK:(n + 1) * CHUNK, :] = (u[n * CHUNK:(n + 1) * CHUNK, :]
                                                * (mixed + bs_ref[...])).astype(BF16)


def _inproj(xs, mod, w):
    nb = xs.shape[0]
    row = lambda width: pl.BlockSpec((None, T, width), lambda b, i: (b, i, 0))
    tab = pl.BlockSpec((T, 128), lambda b, i: (i, 0))
    outs = [jax.ShapeDtypeStruct((nb, ST, width), BF16) for width in (1024, 1024, 512, 256, 256, 256, 256)]
    return pl.pallas_call(
        _inproj_kernel,
        out_shape=outs,
        grid=(nb, NT),
        in_specs=[row(D), _const_spec((MOD_ROWS, 6 * D)), _const_spec((D, ZW)),
                  _const_spec((1, Q_LORA)), _const_spec((Q_LORA, HEADS * 256)), _const_spec((Q_LORA, HEADS * 128)),
                  _const_spec((1, KV_LORA)), _const_spec((KV_LORA, HEADS * NOPE)), _const_spec((KV_LORA, HEADS * DV)),
                  tab, tab, tab, tab,
                  _const_spec((1, CHUNK_W)), _const_spec((1, CHUNK_W)),
                  _const_spec((GROUPS * CHUNK, CHUNK)), _const_spec((CHUNK, CHUNK_W))],
        out_specs=[row(1024), row(1024), row(512), row(256), row(256), row(256), row(256)],
        compiler_params=_params(2),
        name="inproj",
    )(xs, mod, w["win"], w["gq"], w["wuq"], w["wuqc"], w["gkv"], w["wk"], w["wv"],
      w["ckr"], w["skr"], w["cd"], w["sd"], w["lng"], w["lnb"], w["ws"], w["bs"])


def _softmax_step(s, m_ref, l_ref, acc_ref, v):
    m_prev = m_ref[...]
    m_new = jnp.maximum(m_prev, jnp.max(s, axis=-1, keepdims=True))
    alpha = jnp.exp(m_prev - m_new)
    p = jnp.exp(s - m_new)
    l_ref[...] = alpha * l_ref[...] + jnp.sum(p, axis=-1, keepdims=True)
    acc_ref[...] = alpha * acc_ref[...] + _dot(p.astype(BF16), v)
    m_ref[...] = m_new


def _mla_attn_kernel(q_ref, k_ref, v_ref, o_ref, m_ref, l_ref, acc_ref):
    q = q_ref[...]
    m_ref[...] = jnp.full(m_ref.shape, -jnp.inf, F32)
    l_ref[...] = jnp.zeros(l_ref.shape, F32)
    acc_ref[...] = jnp.zeros(acc_ref.shape, F32)

    def step(rows):
        s = lax.dot_general(q, k_ref[rows, :], _NT_DIMS, preferred_element_type=F32)
        _softmax_step(s, m_ref, l_ref, acc_ref, v_ref[rows, :])

    @pl.when(pl.program_id(2) < NT_LAT)
    def _():
        def body(c, carry):
            step(pl.ds(pl.multiple_of(c * TK, TK), TK))
            return carry
        lax.fori_loop(0, S // TK, body, 0)

    step(pl.ds(S, C))
    o_ref[...] = (acc_ref[...] / l_ref[...]).astype(BF16)


def _mla_attention(qm, km, vm, n_q_tiles):
    nb = qm.shape[0]
    return pl.pallas_call(
        _mla_attn_kernel,
        out_shape=jax.ShapeDtypeStruct((nb, n_q_tiles * T, HEADS * DV), BF16),
        grid=(nb, HEADS, n_q_tiles),
        in_specs=[pl.BlockSpec((None, T, 256), lambda b, h, i: (b, i, h)),
                  pl.BlockSpec((None, SK, 256), lambda b, h, i: (b, 0, h)),
                  pl.BlockSpec((None, SK, DV), lambda b, h, i: (b, 0, h))],
        out_specs=pl.BlockSpec((None, T, DV), lambda b, h, i: (b, i, h)),
        scratch_shapes=[pltpu.VMEM((T, 1), F32), pltpu.VMEM((T, 1), F32), pltpu.VMEM((T, DV), F32)],
        compiler_params=_params(3),
        name="mla_attention",
    )(qm, km, vm)


def _diff_attn_kernel(lam_init, q_ref, k_ref, v_ref, lqk_ref, g_ref, o_ref, m_ref, l_ref, acc_ref):
    q = q_ref[...]
    lane = lax.broadcasted_iota(jnp.int32, (T, 128), 1)
    zero = jnp.zeros_like(q)
    qmaps = [jnp.where(lane // DIFF_DQK == j, q, zero) for j in range(4)]
    m_ref[...] = jnp.full(m_ref.shape, -jnp.inf, F32)
    l_ref[...] = jnp.zeros(l_ref.shape, F32)
    acc_ref[...] = jnp.zeros(acc_ref.shape, F32)

    def step(rows):
        k = k_ref[rows, :]
        v = v_ref[rows, :]
        for j in range(4):
            s = lax.dot_general(qmaps[j], k, _NT_DIMS, preferred_element_type=F32)
            _softmax_step(s, m_ref.at[j], l_ref.at[j], acc_ref.at[j], v)

    @pl.when(pl.program_id(2) < NT_LAT)
    def _():
        def body(c, carry):
            step(pl.ds(pl.multiple_of(c * TK, TK), TK))
            return carry
        lax.fori_loop(0, S // TK, body, 0)

    step(pl.ds(S, C))

    lqk = lqk_ref[...]
    lam = (jnp.exp(jnp.sum(lqk[0:1] * lqk[1:2], axis=-1, keepdims=True))
           - jnp.exp(jnp.sum(lqk[2:3] * lqk[3:4], axis=-1, keepdims=True)) + lam_init)
    out = jnp.zeros((T, 128), F32)
    for hh in range(2):
        d = acc_ref[2 * hh] / l_ref[2 * hh] - lam * (acc_ref[2 * hh + 1] / l_ref[2 * hh + 1])
        d = jnp.where(lane // DIFF_DV == hh, d, 0.0)
        ms = jnp.sum(d * d, axis=-1, keepdims=True) * (1.0 / DIFF_DV)
        out = out + d * lax.rsqrt(ms + EPS)
    o_ref[...] = ((out * g_ref[...]) * (1.0 - lam_init)).astype(BF16)


def _diff_attention(qd, kd, vd, lqk, g2, lam_init, n_q_tiles):
    nb = qd.shape[0]
    return pl.pallas_call(
        functools.partial(_diff_attn_kernel, lam_init),
        out_shape=jax.ShapeDtypeStruct((nb, n_q_tiles * T, DIFF_W), BF16),
        grid=(nb, 2, n_q_tiles),
        in_specs=[pl.BlockSpec((None, T, 128), lambda b, h, i: (b, i, h)),
                  pl.BlockSpec((None, SK, 128), lambda b, h, i: (b, 0, h)),
                  pl.BlockSpec((None, SK, 128), lambda b, h, i: (b, 0, h)),
                  pl.BlockSpec((4, DIFF_DQK), lambda b, h, i: (0, 0)),
                  pl.BlockSpec((1, 128), lambda b, h, i: (0, 0))],
        out_specs=pl.BlockSpec((None, T, 128), lambda b, h, i: (b, i, h)),
        scratch_shapes=[pltpu.VMEM((4, T, 1), F32), pltpu.VMEM((4, T, 1), F32), pltpu.VMEM((4, T, 128), F32)],
        compiler_params=_params(3),
        name="diff_attention",
    )(qd, kd, vd, lqk, g2)


def _outproj_kernel(x_ref, om_ref, od_ref, cm_ref, mod_ref, wo_ref, g_ref, b_ref, o_ref):
    r = _mod_row(mod_ref, pl.program_id(1))
    g1 = mod_ref[pl.ds(r, 1), 2 * D:3 * D]
    y = (_dot(om_ref[...], wo_ref[0:512, :]) + _dot(od_ref[...], wo_ref[512:768, :])
         + _dot(cm_ref[...], wo_ref[768:1024, :]))
    o_ref[...] = _ln(DN_ALPHA * x_ref[...] + g1 * y, g_ref[...], b_ref[...])


def _outproj(xs, om, od, cm, mod, w, n_tiles):
    nb = xs.shape[0]
    row = lambda width: pl.BlockSpec((None, T, width), lambda b, i: (b, i, 0))
    return pl.pallas_call(
        _outproj_kernel,
        out_shape=jax.ShapeDtypeStruct((nb, n_tiles * T, D), F32),
        grid=(nb, n_tiles),
        in_specs=[row(D), row(512), row(256), row(256), _const_spec((MOD_ROWS, 6 * D)),
                  _const_spec((D, D)), _const_spec((1, D)), _const_spec((1, D))],
        out_specs=row(D),
        compiler_params=_params(2),
        name="outproj",
    )(xs, om, od, cm, mod, w["wo"], w["ln1g"], w["ln1b"])


def _ffn_kernel(x_ref, xp_ref, xn_ref, mod_ref, wup_ref, cw_ref, cb_ref, wdn_ref, g_ref, b_ref,
                o_ref, h_ref, acc_ref):
    i = pl.program_id(1)
    r = _mod_row(mod_ref, i)
    sh2 = mod_ref[pl.ds(r, 1), 3 * D:4 * D]
    sc2 = mod_ref[pl.ds(r, 1), 4 * D:5 * D]
    g2 = mod_ref[pl.ds(r, 1), 5 * D:6 * D]
    x = x_ref[...]
    modulate = lambda t: t * (1.0 + sc2) + sh2
    seq_start = jnp.logical_or(i == 0, i == NT_LAT)
    seq_end = i >= NT_LAT - 1
    h_ref[0:HALO, :] = jnp.where(seq_start, 0.0, modulate(xp_ref[...])).astype(BF16)
    hm = modulate(x)
    row = lax.broadcasted_iota(jnp.int32, (T, 1), 0)
    hm = jnp.where(jnp.logical_and(i == NT_LAT, row >= C), 0.0, hm)
    h_ref[HALO:HALO + T, :] = hm.astype(BF16)
    h_ref[HALO + T:, :] = jnp.where(seq_end, 0.0, modulate(xn_ref[...])).astype(BF16)
    h = h_ref[...]

    def conv(col):
        up = _dot(h, wup_ref[:, col:col + FF_CHUNK])
        cw = cw_ref[:, col:col + FF_CHUNK]
        prev = pltpu.roll(up, 1, 0)[HALO:HALO + T]
        nxt = pltpu.roll(up, T + 2 * HALO - 1, 0)[HALO:HALO + T]
        return prev * cw[0:1] + up[HALO:HALO + T] * cw[1:2] + nxt * cw[2:3] + cb_ref[:, col:col + FF_CHUNK]

    for j in range(D_FF // FF_CHUNK):
        gate = conv(j * FF_CHUNK)
        val = conv(D_FF + j * FF_CHUNK)
        act = ((gate * (1.0 / (1.0 + jnp.exp(-gate)))) * val).astype(BF16)
        part = _dot(act, wdn_ref[j * FF_CHUNK:(j + 1) * FF_CHUNK, :])
        if j == 0:
            acc_ref[...] = part
        else:
            acc_ref[...] += part
    o_ref[...] = _ln(DN_ALPHA * x + g2 * acc_ref[...], g_ref[...], b_ref[...])


def _ffn(x1, mod, w, n_tiles):
    nb, rows, _ = x1.shape
    nh = rows // HALO
    per = T // HALO
    return pl.pallas_call(
        _ffn_kernel,
        out_shape=jax.ShapeDtypeStruct((nb, n_tiles * T, D), F32),
        grid=(nb, n_tiles),
        in_specs=[pl.BlockSpec((None, T, D), lambda b, i: (b, i, 0)),
                  pl.BlockSpec((None, HALO, D), lambda b, i: (b, jnp.maximum(i * per - 1, 0), 0)),
                  pl.BlockSpec((None, HALO, D), lambda b, i: (b, jnp.minimum((i + 1) * per, nh - 1), 0)),
                  _const_spec((MOD_ROWS, 6 * D)), _const_spec((D, 2 * D_FF)),
                  _const_spec((3, 2 * D_FF)), _const_spec((1, 2 * D_FF)), _const_spec((D_FF, D)),
                  _const_spec((1, D)), _const_spec((1, D))],
        out_specs=pl.BlockSpec((None, T, D), lambda b, i: (b, i, 0)),
        scratch_shapes=[pltpu.VMEM((T + 2 * HALO, D), BF16), pltpu.VMEM((T, D), F32)],
        compiler_params=_params(2),
        name="conv_ffn",
    )(x1, x1, x1, mod, w["wup"], w["convw"], w["convb"], w["wdn"], w["ln2g"], w["ln2b"])


def _swap_halves(n_blocks, half):
    idx = jnp.arange(n_blocks * 2 * half)
    return jnp.where((idx // half) % 2 == 0, idx + half, idx - half)


def _rope_tables():
    t = jnp.arange(S, dtype=jnp.int32)
    rows = (t // GRID_W).astype(F32)[:, None]
    cols = (t % GRID_W).astype(F32)[:, None]

    def tables(n, width):
        inv = ROPE_BASE ** (-jnp.arange(n, dtype=F32) / n)
        ar, ac = rows * inv[None, :], cols * inv[None, :]
        cos = jnp.concatenate([jnp.cos(ar), jnp.cos(ar), jnp.cos(ac), jnp.cos(ac)], -1)
        sin = jnp.concatenate([-jnp.sin(ar), jnp.sin(ar), -jnp.sin(ac), jnp.sin(ac)], -1)
        reps = width // (4 * n)
        cos, sin = jnp.tile(cos, (1, reps)), jnp.tile(sin, (1, reps))
        cos = jnp.concatenate([cos, jnp.ones((ST - S, cos.shape[1]), F32)], 0)
        sin = jnp.concatenate([sin, jnp.zeros((ST - S, sin.shape[1]), F32)], 0)
        return cos, sin

    ckr, skr = tables(ROPE // 4, ROPE)
    pad = jnp.zeros((ST, 128 - ROPE), F32)
    ckr, skr = jnp.concatenate([ckr, pad], -1), jnp.concatenate([skr, pad], -1)
    cd, sd = tables(DIFF_DQK // 4, 128)
    return ckr, skr, cd, sd


def _layer_weights(l, tabs, w_in, mla_gq, mla_wuq, mla_gkv, mla_wukv, diff_lq1, diff_lk1, diff_lq2, diff_lk2,
                   diff_subln_g, sgu_ln_g, sgu_ln_b, sgu_ws, sgu_bs, w_o, ln1_g, ln1_b, ffn_wup, ffn_convw,
                   ffn_convb, ffn_wdown, ln2_g, ln2_b):
    win = w_in[l]
    k_r = win[:, 640:704]
    dq = win[:, 704:960]
    dk = win[:, 960:1216]
    zpad = jnp.zeros((D, 128 - ROPE), F32)
    p64 = _swap_halves(2, ROPE // 4)
    p256 = _swap_halves(16, DIFF_DQK // 4)
    win_wide = jnp.concatenate(
        [win[:, 0:640], k_r, zpad, dq, dk, win[:, 1216:1984],
         k_r[:, p64], zpad, dq[:, p256], dk[:, p256]], axis=1).astype(BF16)

    wuq = mla_wuq[l].reshape(Q_LORA, HEADS, NOPE + ROPE)
    zq = jnp.zeros((Q_LORA, HEADS, 128 - ROPE), F32)
    wuq_main = jnp.concatenate([wuq, zq], -1).reshape(Q_LORA, HEADS * 256).astype(BF16)
    wuq_part = jnp.concatenate([wuq[:, :, NOPE:][:, :, p64], zq], -1).reshape(Q_LORA, HEADS * 128).astype(BF16)
    wukv = mla_wukv[l].reshape(KV_LORA, HEADS, NOPE + DV)
    ckr, skr, cd, sd = tabs
    bias = jnp.repeat(sgu_bs[l].T, CHUNK_W // GROUPS, axis=1)
    return dict(
        win=win_wide, gq=mla_gq[l][None], wuq=wuq_main, wuqc=wuq_part, gkv=mla_gkv[l][None],
        wk=wukv[:, :, :NOPE].reshape(KV_LORA, HEADS * NOPE).astype(BF16),
        wv=wukv[:, :, NOPE:].reshape(KV_LORA, HEADS * DV).astype(BF16),
        ckr=ckr, skr=skr, cd=cd, sd=sd, lng=sgu_ln_g[l][None], lnb=sgu_ln_b[l][None],
        ws=sgu_ws[l].reshape(GROUPS * CHUNK, CHUNK).astype(BF16), bs=bias,
        lqk=jnp.stack([diff_lq1[l], diff_lk1[l], diff_lq2[l], diff_lk2[l]]),
        subg=jnp.tile(diff_subln_g[l], 2)[None],
        wo=w_o[l].astype(BF16), ln1g=ln1_g[l][None], ln1b=ln1_b[l][None],
        wup=ffn_wup[l].astype(BF16), convw=ffn_convw[l], convb=ffn_convb[l][None],
        wdn=ffn_wdown[l].astype(BF16), ln2g=ln2_g[l][None], ln2b=ln2_b[l][None])


def kernel(x, c, ctx, c_ctx, ada_w, ada_b, w_in, mla_gq, mla_wuq, mla_gkv, mla_wukv, diff_lq1, diff_lk1, diff_lq2, diff_lk2, diff_subln_g, sgu_ln_g, sgu_ln_b, sgu_ws, sgu_bs, w_o, ln1_g, ln1_b, ffn_wup, ffn_convw, ffn_convb, ffn_wdown, ln2_g, ln2_b):
    nb = x.shape[0]
    cvec = jnp.concatenate([c, c_ctx[None], jnp.zeros((MOD_ROWS - nb - 1, D), F32)], 0)
    mod = _modulation(cvec, ada_w, ada_b)
    tabs = _rope_tables()
    xs = jnp.concatenate([x, ctx, jnp.zeros((nb, ST - S - C, D), F32)], axis=1)
    for l in range(DEPTH):
        w = _layer_weights(l, tabs, w_in, mla_gq, mla_wuq, mla_gkv, mla_wukv, diff_lq1, diff_lk1, diff_lq2,
                           diff_lk2, diff_subln_g, sgu_ln_g, sgu_ln_b, sgu_ws, sgu_bs, w_o, ln1_g, ln1_b,
                           ffn_wup, ffn_convw, ffn_convb, ffn_wdown, ln2_g, ln2_b)
        n_tiles = NT if l < DEPTH - 1 else NT_LAT
        lam_init = 0.8 - 0.6 * math.exp(-0.3 * l)
        qm, km, vm, qd, kd, vd, cm = _inproj(xs, mod[l], w)
        om = _mla_attention(qm, km, vm, n_tiles)
        od = _diff_attention(qd, kd, vd, w["lqk"], w["subg"], lam_init, n_tiles)
        x1 = _outproj(xs, om, od, cm, mod[l], w, n_tiles)
        xs = _ffn(x1, mod[l], w, n_tiles)
    return xs
```

```python
import functools
import math

import jax
import jax.numpy as jnp
from jax import lax
from jax.experimental import pallas as pl
from jax.experimental.pallas import tpu as pltpu

F32 = jnp.float32
BF16 = jnp.bfloat16

D = 1024
S = 8192
C = 256
GRID_W = 64
DEPTH = 2
T = 512
ST = S + 2 * C
NT = ST // T
NT_LAT = S // T
SK = S + C
HEADS = 4
Q_LORA = 384
KV_LORA = 256
NOPE = 128
ROPE = 64
DV = 128
DIFF_W = 256
DIFF_DQK = 32
DIFF_DV = 64
CHUNK_W = 256
CHUNK = 128
GROUPS = 4
D_FF = 2816
FF_CHUNK = 256
HALO = 16
TK = 512
EPS = 1e-6
ROPE_BASE = 10000.0
DN_ALPHA = (2 * DEPTH) ** 0.25
MLA_SCALE = (NOPE + ROPE) ** -0.5
DIFF_SCALE = DIFF_DQK ** -0.5
LOG2E = math.log2(math.e)
MLA_QSCALE = MLA_SCALE * LOG2E
DIFF_QSCALE = DIFF_SCALE * LOG2E
ZW = 2688
MOD_ROWS = 8
CTX_ROW = 2
VMEM_LIMIT = 56 * 1024 * 1024

_NT_DIMS = (((1,), (1,)), ((), ()))


def _dot(a, b):
    return jnp.dot(a, b, preferred_element_type=F32)


def _rms(x, g):
    return (x * lax.rsqrt(jnp.mean(x * x, axis=-1, keepdims=True) + EPS)) * g


def _ln(x, g, b):
    mu = jnp.mean(x, axis=-1, keepdims=True)
    xc = x - mu
    var = jnp.mean(xc * xc, axis=-1, keepdims=True)
    return (xc * lax.rsqrt(var + EPS)) * g + b


def _params(n_axes):
    return pltpu.CompilerParams(dimension_semantics=("arbitrary",) * n_axes,
                                vmem_limit_bytes=VMEM_LIMIT)


def _const_spec(shape):
    nd = len(shape)
    return pl.BlockSpec(shape, lambda *_: (0,) * nd, pipeline_mode=pl.Buffered(1))


MOD_BLOCK = 1536


def _mod_kernel(c_ref, w_ref, b_ref, o_ref):
    c = c_ref[...]
    sc = (c * (1.0 / (1.0 + jnp.exp(-c)))).astype(BF16)
    o_ref[...] = _dot(sc, w_ref[...].astype(BF16)) + b_ref[...]


def _modulation(cvec, ada_w, ada_b):
    return pl.pallas_call(
        _mod_kernel,
        out_shape=jax.ShapeDtypeStruct((DEPTH, MOD_ROWS, 6 * D), F32),
        grid=(DEPTH, 6 * D // MOD_BLOCK),
        in_specs=[pl.BlockSpec((MOD_ROWS, D), lambda l, j: (0, 0)),
                  pl.BlockSpec((None, D, MOD_BLOCK), lambda l, j: (l, 0, j)),
                  pl.BlockSpec((None, 1, MOD_BLOCK), lambda l, j: (l, 0, j))],
        out_specs=pl.BlockSpec((None, MOD_ROWS, MOD_BLOCK), lambda l, j: (l, 0, j)),
        compiler_params=_params(2),
        name="modulation",
    )(cvec, ada_w, ada_b.reshape(DEPTH, 1, 6 * D))


def _mod_row(mod_ref, tile):
    return jnp.where(tile == NT_LAT, CTX_ROW, pl.program_id(0))


def _inproj_kernel(x_ref, mod_ref, win_ref, gq_ref, wuq_ref, wuqc_ref, gkv_ref, wk_ref, wv_ref,
                   ckr_ref, skr_ref, cd_ref, sd_ref, lng_ref, lnb_ref, ws_ref, bs_ref,
                   qm_ref, km_ref, vmt_ref, qd_ref, kd_ref, vdt_ref, cm_ref):
    r = _mod_row(mod_ref, pl.program_id(1))
    sh1 = mod_ref[pl.ds(r, 1), 0:D]
    sc1 = mod_ref[pl.ds(r, 1), D:2 * D]
    h = (x_ref[...] * (1.0 + sc1) + sh1).astype(BF16)
    z = _dot(h, win_ref[...])

    ckr = ckr_ref[...]
    skr = skr_ref[...]
    cd = cd_ref[...]
    sd = sd_ref[...]

    qn = _rms(z[:, 0:Q_LORA], gq_ref[...]).astype(BF16)
    qa = _dot(qn, wuq_ref[...])
    qc = _dot(qn, wuqc_ref[...])
    for hd in range(HEADS):
        nope = qa[:, hd * 256:hd * 256 + NOPE]
        rot = qa[:, hd * 256 + NOPE:(hd + 1) * 256] * ckr + qc[:, hd * 128:(hd + 1) * 128] * skr
        qm_ref[:, hd * 256:hd * 256 + NOPE] = (nope * MLA_QSCALE).astype(BF16)
        qm_ref[:, hd * 256 + NOPE:(hd + 1) * 256] = (rot * MLA_QSCALE).astype(BF16)

    kvn = _rms(z[:, Q_LORA:Q_LORA + KV_LORA], gkv_ref[...]).astype(BF16)
    kn = _dot(kvn, wk_ref[...])
    vmt_ref[...] = _dot(kvn, wv_ref[...]).T.astype(BF16)
    kr = (z[:, 640:768] * ckr + z[:, 2048:2176] * skr).astype(BF16)
    for hd in range(HEADS):
        km_ref[:, hd * 256:hd * 256 + NOPE] = kn[:, hd * 128:(hd + 1) * 128].astype(BF16)
        km_ref[:, hd * 256 + NOPE:(hd + 1) * 256] = kr

    for blk in range(2):
        lo = blk * 128
        qd_ref[:, lo:lo + 128] = ((z[:, 768 + lo:896 + lo] * cd + z[:, 2176 + lo:2304 + lo] * sd)
                                  * DIFF_QSCALE).astype(BF16)
        kd_ref[:, lo:lo + 128] = (z[:, 1024 + lo:1152 + lo] * cd
                                  + z[:, 2432 + lo:2560 + lo] * sd).astype(BF16)
    vdt_ref[...] = z[:, 1280:1536].T.astype(BF16)

    zc = z[:, 1536:2048]
    zc = 0.5 * zc * (1.0 + jnp.tanh(math.sqrt(2.0 / math.pi) * (zc + 0.044715 * (zc * zc * zc))))
    u = zc[:, :CHUNK_W]
    v = _ln(zc[:, CHUNK_W:], lng_ref[...], lnb_ref[...]).astype(BF16)
    group = lax.broadcasted_iota(jnp.int32, (CHUNK, CHUNK_W), 1) // (CHUNK_W // GROUPS)
    ws = ws_ref[...]
    for n in range(T // CHUNK):
        mixed_all = _dot(ws, v[n * CHUNK:(n + 1) * CHUNK, :])
        mixed = mixed_all[0:CHUNK]
        for g in range(1, GROUPS):
            mixed = jnp.where(group == g, mixed_all[g * CHUNK:(g + 1) * CHUNK], mixed)
        cm_ref[n * CHUNK:(n + 1) * CHUNK, :] = (u[n * CHUNK:(n + 1) * CHUNK, :]
                                                * (mixed + bs_ref[...])).astype(BF16)


def _inproj(xs, mod, w):
    nb = xs.shape[0]
    row = lambda width: pl.BlockSpec((None, T, width), lambda b, i: (b, i, 0))
    tab = pl.BlockSpec((T, 128), lambda b, i: (i, 0))
    rows = lambda width: jax.ShapeDtypeStruct((nb, ST, width), BF16)
    cols = lambda width: jax.ShapeDtypeStruct((nb, NT, width, T), BF16)
    col = lambda width: pl.BlockSpec((None, None, width, T), lambda b, i: (b, i, 0, 0))
    outs = [rows(1024), rows(1024), cols(HEADS * DV), rows(256), rows(256), cols(DIFF_W), rows(256)]
    return pl.pallas_call(
        _inproj_kernel,
        out_shape=outs,
        grid=(nb, NT),
        in_specs=[row(D), _const_spec((MOD_ROWS, 6 * D)), _const_spec((D, ZW)),
                  _const_spec((1, Q_LORA)), _const_spec((Q_LORA, HEADS * 256)), _const_spec((Q_LORA, HEADS * 128)),
                  _const_spec((1, KV_LORA)), _const_spec((KV_LORA, HEADS * NOPE)), _const_spec((KV_LORA, HEADS * DV)),
                  tab, tab, tab, tab,
                  _const_spec((1, CHUNK_W)), _const_spec((1, CHUNK_W)),
                  _const_spec((GROUPS * CHUNK, CHUNK)), _const_spec((CHUNK, CHUNK_W))],
        out_specs=[row(1024), row(1024), col(HEADS * DV), row(256), row(256), col(DIFF_W), row(256)],
        compiler_params=_params(2),
        name="inproj",
    )(xs, mod, w["win"], w["gq"], w["wuq"], w["wuqc"], w["gkv"], w["wk"], w["wv"],
      w["ckr"], w["skr"], w["cd"], w["sd"], w["lng"], w["lnb"], w["ws"], w["bs"])


def _key_rows(c):
    if isinstance(c, int):
        return pl.ds(c * TK, TK)
    return pl.ds(pl.multiple_of(c * TK, TK), TK)


def _attend(n, qk_lat, qk_ctx, v_lat, v_ctx, s0_ref, s1_ref, m_ref, l_ref, acc_ref, with_latents):
    for j, s in enumerate(qk_ctx()):
        m = jnp.max(s, axis=0, keepdims=True)
        p = jnp.exp2(s - m)
        m_ref[j] = m
        l_ref[j] = jnp.sum(p, axis=0, keepdims=True)
        acc_ref[j] = _dot(v_ctx(j), p.astype(BF16))

    def scores(s_ref, c):
        for j, s in enumerate(qk_lat(c)):
            s_ref[j] = s

    def update(s_ref, c):
        for j in range(n):
            s = s_ref[j]
            m_prev = m_ref[j]
            m_new = jnp.maximum(m_prev, jnp.max(s, axis=0, keepdims=True))
            alpha = jnp.exp2(m_prev - m_new)
            p = jnp.exp2(s - m_new)
            l_ref[j] = alpha * l_ref[j] + jnp.sum(p, axis=0, keepdims=True)
            acc_ref[j] = alpha * acc_ref[j] + _dot(v_lat(j, c), p.astype(BF16))
            m_ref[j] = m_new

    @pl.when(with_latents)
    def _():
        n_chunks = S // TK
        scores(s0_ref, 0)

        def body(cc, carry):
            a = 2 * cc
            scores(s1_ref, a + 1)
            update(s0_ref, a)
            scores(s0_ref, a + 2)
            update(s1_ref, a + 1)
            return carry

        lax.fori_loop(0, n_chunks // 2 - 1, body, 0)
        scores(s1_ref, n_chunks - 1)
        update(s0_ref, n_chunks - 2)
        update(s1_ref, n_chunks - 1)


def _mla_attn_kernel(q_ref, k_ref, vt_ref, o_ref, s0_ref, s1_ref, m_ref, l_ref, acc_ref):
    q = q_ref[...]
    qk = lambda rows: [lax.dot_general(k_ref[rows, :], q, _NT_DIMS, preferred_element_type=F32)]
    _attend(1, lambda c: qk(_key_rows(c)), lambda: qk(pl.ds(S, C)),
            lambda j, c: vt_ref[c], lambda j: vt_ref[NT_LAT, :, 0:C],
            s0_ref, s1_ref, m_ref, l_ref, acc_ref, pl.program_id(2) < NT_LAT)
    o_ref[...] = (acc_ref[0] / l_ref[0]).T.astype(BF16)


def _mla_attention(qm, km, vmt, n_q_tiles):
    nb = qm.shape[0]
    return pl.pallas_call(
        _mla_attn_kernel,
        out_shape=jax.ShapeDtypeStruct((nb, n_q_tiles * T, HEADS * DV), BF16),
        grid=(nb, HEADS, n_q_tiles),
        in_specs=[pl.BlockSpec((None, T, 256), lambda b, h, i: (b, i, h)),
                  pl.BlockSpec((None, SK, 256), lambda b, h, i: (b, 0, h)),
                  pl.BlockSpec((None, NT, DV, T), lambda b, h, i: (b, 0, h, 0))],
        out_specs=pl.BlockSpec((None, T, DV), lambda b, h, i: (b, i, h)),
        scratch_shapes=[pltpu.VMEM((1, TK, T), F32), pltpu.VMEM((1, TK, T), F32),
                        pltpu.VMEM((1, 1, T), F32), pltpu.VMEM((1, 1, T), F32), pltpu.VMEM((1, DV, T), F32)],
        compiler_params=_params(3),
        name="mla_attention",
    )(qm, km, vmt)


def _diff_attn_kernel(lam_init, q_ref, k_ref, vt_ref, lqk_ref, g_ref, o_ref, s0_ref, s1_ref, m_ref, l_ref, acc_ref):
    q = q_ref[...]
    lane = lax.broadcasted_iota(jnp.int32, (T, 128), 1)
    zero = jnp.zeros_like(q)
    qmaps = [jnp.where(lane // DIFF_DQK == j, q, zero) for j in range(4)]

    def qk(rows):
        k = k_ref[rows, :]
        return [lax.dot_general(k, qmaps[j], _NT_DIMS, preferred_element_type=F32) for j in range(4)]

    head_rows = lambda j: pl.ds((j // 2) * DIFF_DV, DIFF_DV)
    _attend(4, lambda c: qk(_key_rows(c)), lambda: qk(pl.ds(S, C)),
            lambda j, c: vt_ref[c, head_rows(j), :], lambda j: vt_ref[NT_LAT, head_rows(j), 0:C],
            s0_ref, s1_ref, m_ref, l_ref, acc_ref, pl.program_id(2) < NT_LAT)

    lqk = lqk_ref[...]
    lam = (jnp.exp(jnp.sum(lqk[0:1] * lqk[1:2], axis=-1, keepdims=True))
           - jnp.exp(jnp.sum(lqk[2:3] * lqk[3:4], axis=-1, keepdims=True)) + lam_init)
    heads = []
    for hh in range(2):
        d = acc_ref[2 * hh] / l_ref[2 * hh] - lam * (acc_ref[2 * hh + 1] / l_ref[2 * hh + 1])
        ms = jnp.mean(d * d, axis=0, keepdims=True)
        heads.append(d * lax.rsqrt(ms + EPS))
    out = jnp.concatenate(heads, axis=0).T
    o_ref[...] = ((out * g_ref[...]) * (1.0 - lam_init)).astype(BF16)


def _diff_attention(qd, kd, vdt, lqk, g2, lam_init, n_q_tiles):
    nb = qd.shape[0]
    return pl.pallas_call(
        functools.partial(_diff_attn_kernel, lam_init),
        out_shape=jax.ShapeDtypeStruct((nb, n_q_tiles * T, DIFF_W), BF16),
        grid=(nb, 2, n_q_tiles),
        in_specs=[pl.BlockSpec((None, T, 128), lambda b, h, i: (b, i, h)),
                  pl.BlockSpec((None, SK, 128), lambda b, h, i: (b, 0, h)),
                  pl.BlockSpec((None, NT, 128, T), lambda b, h, i: (b, 0, h, 0)),
                  pl.BlockSpec((4, DIFF_DQK), lambda b, h, i: (0, 0)),
                  pl.BlockSpec((1, 128), lambda b, h, i: (0, 0))],
        out_specs=pl.BlockSpec((None, T, 128), lambda b, h, i: (b, i, h)),
        scratch_shapes=[pltpu.VMEM((4, TK, T), F32), pltpu.VMEM((4, TK, T), F32),
                        pltpu.VMEM((4, 1, T), F32), pltpu.VMEM((4, 1, T), F32),
                        pltpu.VMEM((4, DIFF_DV, T), F32)],
        compiler_params=_params(3),
        name="diff_attention",
    )(qd, kd, vdt, lqk, g2)


def _outproj_kernel(x_ref, om_ref, od_ref, cm_ref, mod_ref, wo_ref, g_ref, b_ref, o_ref):
    r = _mod_row(mod_ref, pl.program_id(1))
    g1 = mod_ref[pl.ds(r, 1), 2 * D:3 * D]
    y = (_dot(om_ref[...], wo_ref[0:512, :]) + _dot(od_ref[...], wo_ref[512:768, :])
         + _dot(cm_ref[...], wo_ref[768:1024, :]))
    o_ref[...] = _ln(DN_ALPHA * x_ref[...] + g1 * y, g_ref[...], b_ref[...])


def _outproj(xs, om, od, cm, mod, w, n_tiles):
    nb = xs.shape[0]
    row = lambda width: pl.BlockSpec((None, T, width), lambda b, i: (b, i, 0))
    return pl.pallas_call(
        _outproj_kernel,
        out_shape=jax.ShapeDtypeStruct((nb, n_tiles * T, D), F32),
        grid=(nb, n_tiles),
        in_specs=[row(D), row(512), row(256), row(256), _const_spec((MOD_ROWS, 6 * D)),
                  _const_spec((D, D)), _const_spec((1, D)), _const_spec((1, D))],
        out_specs=row(D),
        compiler_params=_params(2),
        name="outproj",
    )(xs, om, od, cm, mod, w["wo"], w["ln1g"], w["ln1b"])


def _ffn_kernel(x_ref, xp_ref, xn_ref, mod_ref, wup_ref, cw_ref, cb_ref, wdn_ref, g_ref, b_ref,
                o_ref, h_ref, acc_ref):
    i = pl.program_id(1)
    r = _mod_row(mod_ref, i)
    sh2 = mod_ref[pl.ds(r, 1), 3 * D:4 * D]
    sc2 = mod_ref[pl.ds(r, 1), 4 * D:5 * D]
    g2 = mod_ref[pl.ds(r, 1), 5 * D:6 * D]
    x = x_ref[...]
    modulate = lambda t: t * (1.0 + sc2) + sh2
    seq_start = jnp.logical_or(i == 0, i == NT_LAT)
    seq_end = i >= NT_LAT - 1
    h_ref[0:HALO, :] = jnp.where(seq_start, 0.0, modulate(xp_ref[...])).astype(BF16)
    hm = modulate(x)
    row = lax.broadcasted_iota(jnp.int32, (T, 1), 0)
    hm = jnp.where(jnp.logical_and(i == NT_LAT, row >= C), 0.0, hm)
    h_ref[HALO:HALO + T, :] = hm.astype(BF16)
    h_ref[HALO + T:, :] = jnp.where(seq_end, 0.0, modulate(xn_ref[...])).astype(BF16)
    h = h_ref[...]

    def conv(col):
        up = _dot(h, wup_ref[:, col:col + FF_CHUNK])
        cw = cw_ref[:, col:col + FF_CHUNK]
        prev = pltpu.roll(up, 1, 0)[HALO:HALO + T]
        nxt = pltpu.roll(up, T + 2 * HALO - 1, 0)[HALO:HALO + T]
        return prev * cw[0:1] + up[HALO:HALO + T] * cw[1:2] + nxt * cw[2:3] + cb_ref[:, col:col + FF_CHUNK]

    for j in range(D_FF // FF_CHUNK):
        gate = conv(j * FF_CHUNK)
        val = conv(D_FF + j * FF_CHUNK)
        act = ((gate * (1.0 / (1.0 + jnp.exp(-gate)))) * val).astype(BF16)
        part = _dot(act, wdn_ref[j * FF_CHUNK:(j + 1) * FF_CHUNK, :])
        if j == 0:
            acc_ref[...] = part
        else:
            acc_ref[...] += part
    o_ref[...] = _ln(DN_ALPHA * x + g2 * acc_ref[...], g_ref[...], b_ref[...])


def _ffn(x1, mod, w, n_tiles):
    nb, rows, _ = x1.shape
    nh = rows // HALO
    per = T // HALO
    return pl.pallas_call(
        _ffn_kernel,
        out_shape=jax.ShapeDtypeStruct((nb, n_tiles * T, D), F32),
        grid=(nb, n_tiles),
        in_specs=[pl.BlockSpec((None, T, D), lambda b, i: (b, i, 0)),
                  pl.BlockSpec((None, HALO, D), lambda b, i: (b, jnp.maximum(i * per - 1, 0), 0)),
                  pl.BlockSpec((None, HALO, D), lambda b, i: (b, jnp.minimum((i + 1) * per, nh - 1), 0)),
                  _const_spec((MOD_ROWS, 6 * D)), _const_spec((D, 2 * D_FF)),
                  _const_spec((3, 2 * D_FF)), _const_spec((1, 2 * D_FF)), _const_spec((D_FF, D)),
                  _const_spec((1, D)), _const_spec((1, D))],
        out_specs=pl.BlockSpec((None, T, D), lambda b, i: (b, i, 0)),
        scratch_shapes=[pltpu.VMEM((T + 2 * HALO, D), BF16), pltpu.VMEM((T, D), F32)],
        compiler_params=_params(2),
        name="conv_ffn",
    )(x1, x1, x1, mod, w["wup"], w["convw"], w["convb"], w["wdn"], w["ln2g"], w["ln2b"])


def _swap_halves(n_blocks, half):
    idx = jnp.arange(n_blocks * 2 * half)
    return jnp.where((idx // half) % 2 == 0, idx + half, idx - half)


def _rope_tables():
    t = jnp.arange(S, dtype=jnp.int32)
    rows = (t // GRID_W).astype(F32)[:, None]
    cols = (t % GRID_W).astype(F32)[:, None]

    def tables(n, width):
        inv = ROPE_BASE ** (-jnp.arange(n, dtype=F32) / n)
        ar, ac = rows * inv[None, :], cols * inv[None, :]
        cos = jnp.concatenate([jnp.cos(ar), jnp.cos(ar), jnp.cos(ac), jnp.cos(ac)], -1)
        sin = jnp.concatenate([-jnp.sin(ar), jnp.sin(ar), -jnp.sin(ac), jnp.sin(ac)], -1)
        reps = width // (4 * n)
        cos, sin = jnp.tile(cos, (1, reps)), jnp.tile(sin, (1, reps))
        cos = jnp.concatenate([cos, jnp.ones((ST - S, cos.shape[1]), F32)], 0)
        sin = jnp.concatenate([sin, jnp.zeros((ST - S, sin.shape[1]), F32)], 0)
        return cos, sin

    ckr, skr = tables(ROPE // 4, ROPE)
    pad = jnp.zeros((ST, 128 - ROPE), F32)
    ckr, skr = jnp.concatenate([ckr, pad], -1), jnp.concatenate([skr, pad], -1)
    cd, sd = tables(DIFF_DQK // 4, 128)
    return ckr, skr, cd, sd


def _layer_weights(l, tabs, w_in, mla_gq, mla_wuq, mla_gkv, mla_wukv, diff_lq1, diff_lk1, diff_lq2, diff_lk2,
                   diff_subln_g, sgu_ln_g, sgu_ln_b, sgu_ws, sgu_bs, w_o, ln1_g, ln1_b, ffn_wup, ffn_convw,
                   ffn_convb, ffn_wdown, ln2_g, ln2_b):
    win = w_in[l]
    k_r = win[:, 640:704]
    dq = win[:, 704:960]
    dk = win[:, 960:1216]
    zpad = jnp.zeros((D, 128 - ROPE), F32)
    p64 = _swap_halves(2, ROPE // 4)
    p256 = _swap_halves(16, DIFF_DQK // 4)
    win_wide = jnp.concatenate(
        [win[:, 0:640], k_r, zpad, dq, dk, win[:, 1216:1984],
         k_r[:, p64], zpad, dq[:, p256], dk[:, p256]], axis=1).astype(BF16)

    wuq = mla_wuq[l].reshape(Q_LORA, HEADS, NOPE + ROPE)
    zq = jnp.zeros((Q_LORA, HEADS, 128 - ROPE), F32)
    wuq_main = jnp.concatenate([wuq, zq], -1).reshape(Q_LORA, HEADS * 256).astype(BF16)
    wuq_part = jnp.concatenate([wuq[:, :, NOPE:][:, :, p64], zq], -1).reshape(Q_LORA, HEADS * 128).astype(BF16)
    wukv = mla_wukv[l].reshape(KV_LORA, HEADS, NOPE + DV)
    ckr, skr, cd, sd = tabs
    bias = jnp.repeat(sgu_bs[l].T, CHUNK_W // GROUPS, axis=1)
    return dict(
        win=win_wide, gq=mla_gq[l][None], wuq=wuq_main, wuqc=wuq_part, gkv=mla_gkv[l][None],
        wk=wukv[:, :, :NOPE].reshape(KV_LORA, HEADS * NOPE).astype(BF16),
        wv=wukv[:, :, NOPE:].reshape(KV_LORA, HEADS * DV).astype(BF16),
        ckr=ckr, skr=skr, cd=cd, sd=sd, lng=sgu_ln_g[l][None], lnb=sgu_ln_b[l][None],
        ws=sgu_ws[l].reshape(GROUPS * CHUNK, CHUNK).astype(BF16), bs=bias,
        lqk=jnp.stack([diff_lq1[l], diff_lk1[l], diff_lq2[l], diff_lk2[l]]),
        subg=jnp.tile(diff_subln_g[l], 2)[None],
        wo=w_o[l].astype(BF16), ln1g=ln1_g[l][None], ln1b=ln1_b[l][None],
        wup=ffn_wup[l].astype(BF16), convw=ffn_convw[l], convb=ffn_convb[l][None],
        wdn=ffn_wdown[l].astype(BF16), ln2g=ln2_g[l][None], ln2b=ln2_b[l][None])


def kernel(x, c, ctx, c_ctx, ada_w, ada_b, w_in, mla_gq, mla_wuq, mla_gkv, mla_wukv, diff_lq1, diff_lk1, diff_lq2, diff_lk2, diff_subln_g, sgu_ln_g, sgu_ln_b, sgu_ws, sgu_bs, w_o, ln1_g, ln1_b, ffn_wup, ffn_convw, ffn_convb, ffn_wdown, ln2_g, ln2_b):
    nb = x.shape[0]
    cvec = jnp.concatenate([c, c_ctx[None], jnp.zeros((MOD_ROWS - nb - 1, D), F32)], 0)
    mod = _modulation(cvec, ada_w, ada_b)
    tabs = _rope_tables()
    xs = jnp.concatenate([x, ctx, jnp.zeros((nb, ST - S - C, D), F32)], axis=1)
    for l in range(DEPTH):
        w = _layer_weights(l, tabs, w_in, mla_gq, mla_wuq, mla_gkv, mla_wukv, diff_lq1, diff_lk1, diff_lq2,
                           diff_lk2, diff_subln_g, sgu_ln_g, sgu_ln_b, sgu_ws, sgu_bs, w_o, ln1_g, ln1_b,
                           ffn_wup, ffn_convw, ffn_convb, ffn_wdown, ln2_g, ln2_b)
        n_tiles = NT if l < DEPTH - 1 else NT_LAT
        lam_init = 0.8 - 0.6 * math.exp(-0.3 * l)
        qm, km, vmt, qd, kd, vdt, cm = _inproj(xs, mod[l], w)
        om = _mla_attention(qm, km, vmt, n_tiles)
        od = _diff_attention(qd, kd, vdt, w["lqk"], w["subg"], lam_init, n_tiles)
        x1 = _outproj(xs, om, od, cm, mod[l], w, n_tiles)
        xs = _ffn(x1, mod[l], w, n_tiles)
    return xs
```

```python
import functools
import math

import jax
import jax.numpy as jnp
from jax import lax
from jax.experimental import pallas as pl
from jax.experimental.pallas import tpu as pltpu

F32 = jnp.float32
BF16 = jnp.bfloat16

D = 1024
S = 8192
C = 256
GRID_W = 64
DEPTH = 2
T = 512
ST = S + 2 * C
NT = ST // T
NT_LAT = S // T
HEADS = 4
Q_LORA = 384
KV_LORA = 256
NOPE = 128
ROPE = 64
DV = 128
DIFF_W = 256
DIFF_DQK = 32
DIFF_DV = 64
CHUNK_W = 256
CHUNK = 128
GROUPS = 4
D_FF = 2816
FF_CHUNK = 256
HALO = 16
ONES_ROWS = 16
MLA_VROWS = DV + ONES_ROWS
DIFF_VROWS = DIFF_DV + ONES_ROWS
EPS = 1e-6
ROPE_BASE = 10000.0
DN_ALPHA = (2 * DEPTH) ** 0.25
MLA_SCALE = (NOPE + ROPE) ** -0.5
DIFF_SCALE = DIFF_DQK ** -0.5
LOG2E = math.log2(math.e)
MLA_QSCALE = MLA_SCALE * LOG2E
DIFF_QSCALE = DIFF_SCALE * LOG2E
ZW = 2688
MOD_ROWS = 8
CTX_ROW = 2
VMEM_LIMIT = 56 * 1024 * 1024

_NT_DIMS = (((1,), (1,)), ((), ()))


def _dot(a, b):
    return jnp.dot(a, b, preferred_element_type=F32)


def _rms(x, g):
    return (x * lax.rsqrt(jnp.mean(x * x, axis=-1, keepdims=True) + EPS)) * g


def _ln(x, g, b):
    mu = jnp.mean(x, axis=-1, keepdims=True)
    xc = x - mu
    var = jnp.mean(xc * xc, axis=-1, keepdims=True)
    return (xc * lax.rsqrt(var + EPS)) * g + b


def _params(n_axes):
    return pltpu.CompilerParams(dimension_semantics=("arbitrary",) * n_axes,
                                vmem_limit_bytes=VMEM_LIMIT)


def _const_spec(shape):
    nd = len(shape)
    return pl.BlockSpec(shape, lambda *_: (0,) * nd, pipeline_mode=pl.Buffered(1))


MOD_BLOCK = 1536


def _mod_kernel(c_ref, w_ref, b_ref, o_ref):
    c = c_ref[...]
    sc = (c * (1.0 / (1.0 + jnp.exp(-c)))).astype(BF16)
    o_ref[...] = _dot(sc, w_ref[...].astype(BF16)) + b_ref[...]


def _modulation(cvec, ada_w, ada_b):
    return pl.pallas_call(
        _mod_kernel,
        out_shape=jax.ShapeDtypeStruct((DEPTH, MOD_ROWS, 6 * D), F32),
        grid=(DEPTH, 6 * D // MOD_BLOCK),
        in_specs=[pl.BlockSpec((MOD_ROWS, D), lambda l, j: (0, 0)),
                  pl.BlockSpec((None, D, MOD_BLOCK), lambda l, j: (l, 0, j)),
                  pl.BlockSpec((None, 1, MOD_BLOCK), lambda l, j: (l, 0, j))],
        out_specs=pl.BlockSpec((None, MOD_ROWS, MOD_BLOCK), lambda l, j: (l, 0, j)),
        compiler_params=_params(2),
        name="modulation",
    )(cvec, ada_w, ada_b.reshape(DEPTH, 1, 6 * D))


def _mod_row(tile):
    return jnp.where(tile == NT_LAT, CTX_ROW, pl.program_id(0))


def _inproj_kernel(x_ref, mod_ref, win_ref, gq_ref, wuq_ref, wuqc_ref, gkv_ref, wk_ref, wv_ref,
                   ckr_ref, skr_ref, cd_ref, sd_ref, lng_ref, lnb_ref, ws_ref, bs_ref,
                   qm_ref, km_ref, vmt_ref, qd_ref, kd_ref, vdt_ref, cm_ref):
    r = _mod_row(pl.program_id(1))
    sh1 = mod_ref[pl.ds(r, 1), 0:D]
    sc1 = mod_ref[pl.ds(r, 1), D:2 * D]
    h = (x_ref[...] * (1.0 + sc1) + sh1).astype(BF16)
    z = _dot(h, win_ref[...])

    ckr = ckr_ref[...]
    skr = skr_ref[...]
    cd = cd_ref[...]
    sd = sd_ref[...]

    qn = _rms(z[:, 0:Q_LORA], gq_ref[...]).astype(BF16)
    qa = _dot(qn, wuq_ref[...])
    qc = _dot(qn, wuqc_ref[...])
    for hd in range(HEADS):
        nope = qa[:, hd * 256:hd * 256 + NOPE]
        rot = qa[:, hd * 256 + NOPE:(hd + 1) * 256] * ckr + qc[:, hd * 128:(hd + 1) * 128] * skr
        qm_ref[:, hd * 256:hd * 256 + NOPE] = (nope * MLA_QSCALE).astype(BF16)
        qm_ref[:, hd * 256 + NOPE:(hd + 1) * 256] = (rot * MLA_QSCALE).astype(BF16)

    kvn = _rms(z[:, Q_LORA:Q_LORA + KV_LORA], gkv_ref[...]).astype(BF16)
    kn = _dot(kvn, wk_ref[...])
    ones = jnp.ones((ONES_ROWS, T), BF16)
    vt = _dot(kvn, wv_ref[...]).T.astype(BF16)
    for hd in range(HEADS):
        vmt_ref[hd * MLA_VROWS:hd * MLA_VROWS + DV, :] = vt[hd * DV:(hd + 1) * DV]
        vmt_ref[hd * MLA_VROWS + DV:(hd + 1) * MLA_VROWS, :] = ones
    kr = (z[:, 640:768] * ckr + z[:, 2048:2176] * skr).astype(BF16)
    for hd in range(HEADS):
        km_ref[:, hd * 256:hd * 256 + NOPE] = kn[:, hd * 128:(hd + 1) * 128].astype(BF16)
        km_ref[:, hd * 256 + NOPE:(hd + 1) * 256] = kr

    for blk in range(2):
        lo = blk * 128
        qd_ref[:, lo:lo + 128] = ((z[:, 768 + lo:896 + lo] * cd + z[:, 2176 + lo:2304 + lo] * sd)
                                  * DIFF_QSCALE).astype(BF16)
        kd_ref[:, lo:lo + 128] = (z[:, 1024 + lo:1152 + lo] * cd
                                  + z[:, 2432 + lo:2560 + lo] * sd).astype(BF16)
    vt = z[:, 1280:1536].T.astype(BF16)
    for hd in range(HEADS):
        vdt_ref[hd * DIFF_VROWS:hd * DIFF_VROWS + DIFF_DV, :] = vt[hd * DIFF_DV:(hd + 1) * DIFF_DV]
        vdt_ref[hd * DIFF_VROWS + DIFF_DV:(hd + 1) * DIFF_VROWS, :] = ones

    zc = z[:, 1536:2048]
    zc = 0.5 * zc * (1.0 + jnp.tanh(math.sqrt(2.0 / math.pi) * (zc + 0.044715 * (zc * zc * zc))))
    u = zc[:, :CHUNK_W]
    v = _ln(zc[:, CHUNK_W:], lng_ref[...], lnb_ref[...]).astype(BF16)
    group = lax.broadcasted_iota(jnp.int32, (CHUNK, CHUNK_W), 1) // (CHUNK_W // GROUPS)
    ws = ws_ref[...]
    for n in range(T // CHUNK):
        mixed_all = _dot(ws, v[n * CHUNK:(n + 1) * CHUNK, :])
        mixed = mixed_all[0:CHUNK]
        for g in range(1, GROUPS):
            mixed = jnp.where(group == g, mixed_all[g * CHUNK:(g + 1) * CHUNK], mixed)
        cm_ref[n * CHUNK:(n + 1) * CHUNK, :] = (u[n * CHUNK:(n + 1) * CHUNK, :]
                                                * (mixed + bs_ref[...])).astype(BF16)


def _inproj(xs, mod, w):
    nb = xs.shape[0]
    row = lambda width: pl.BlockSpec((None, T, width), lambda b, i: (b, i, 0))
    tab = pl.BlockSpec((T, 128), lambda b, i: (i, 0))
    rows = lambda width: jax.ShapeDtypeStruct((nb, ST, width), BF16)
    cols = lambda width: jax.ShapeDtypeStruct((nb, NT, width, T), BF16)
    col = lambda width: pl.BlockSpec((None, None, width, T), lambda b, i: (b, i, 0, 0))
    outs = [rows(1024), rows(1024), cols(HEADS * MLA_VROWS), rows(256), rows(256), cols(HEADS * DIFF_VROWS),
            rows(256)]
    return pl.pallas_call(
        _inproj_kernel,
        out_shape=outs,
        grid=(nb, NT),
        in_specs=[row(D), _const_spec((MOD_ROWS, 6 * D)), _const_spec((D, ZW)),
                  _const_spec((1, Q_LORA)), _const_spec((Q_LORA, HEADS * 256)), _const_spec((Q_LORA, HEADS * 128)),
                  _const_spec((1, KV_LORA)), _const_spec((KV_LORA, HEADS * NOPE)), _const_spec((KV_LORA, HEADS * DV)),
                  tab, tab, tab, tab,
                  _const_spec((1, CHUNK_W)), _const_spec((1, CHUNK_W)),
                  _const_spec((GROUPS * CHUNK, CHUNK)), _const_spec((CHUNK, CHUNK_W))],
        out_specs=[row(1024), row(1024), col(HEADS * MLA_VROWS), row(256), row(256), col(HEADS * DIFF_VROWS),
                   row(256)],
        compiler_params=_params(2),
        name="inproj",
    )(xs, mod, w["win"], w["gq"], w["wuq"], w["wuqc"], w["gkv"], w["wk"], w["wv"],
      w["ckr"], w["skr"], w["cd"], w["sd"], w["lng"], w["lnb"], w["ws"], w["bs"])


def _key_rows(c):
    if isinstance(c, int):
        return pl.ds(c * T, T)
    return pl.ds(pl.multiple_of(c * T, T), T)


def _attend(n, qk, vt, s_ref, m_ref, acc_ref, only_context):
    m_ref[...] = jnp.full(m_ref.shape, -jnp.inf, F32)
    acc_ref[...] = jnp.zeros(acc_ref.shape, F32)

    def scores(j, c):
        s_ref[j] = qk(j, c)

    def update(j, c, mask_padding=False):
        s = s_ref[j]
        if mask_padding:
            key = lax.broadcasted_iota(jnp.int32, (T, T), 0)
            s = jnp.where(key < C, s, -jnp.inf)
        m_prev = m_ref[j]
        m_new = jnp.maximum(m_prev, jnp.max(s, axis=0, keepdims=True))
        alpha = jnp.exp2(m_prev - m_new)
        p = jnp.exp2(s - m_new).astype(BF16)
        acc_ref[j] = alpha * acc_ref[j] + _dot(vt(j, c), p)
        m_ref[j] = m_new

    @pl.when(jnp.logical_not(only_context))
    def _():
        for j in range(n - 1):
            scores(j, 0)

        def body(c, carry):
            scores(n - 1, c)
            for j in range(n - 1):
                update(j, c)
                scores(j, c + 1)
            update(n - 1, c)
            return carry

        lax.fori_loop(0, NT_LAT, body, 0)

    @pl.when(only_context)
    def _():
        for j in range(n - 1):
            scores(j, NT_LAT)

    scores(n - 1, NT_LAT)
    for j in range(n):
        update(j, NT_LAT, mask_padding=True)


def _mla_attn_kernel(q_ref, k_ref, vt_ref, o_ref, s_ref, m_ref, acc_ref):
    def qk(j, c):
        return lax.dot_general(k_ref[_key_rows(c), j * 256:(j + 1) * 256], q_ref[:, j * 256:(j + 1) * 256],
                               _NT_DIMS, preferred_element_type=F32)

    _attend(HEADS, qk, lambda j, c: vt_ref[c, j * MLA_VROWS:(j + 1) * MLA_VROWS, :],
            s_ref, m_ref, acc_ref, pl.program_id(1) == NT_LAT)
    for hd in range(HEADS):
        o = acc_ref[hd, 0:DV, :] / acc_ref[hd, DV:DV + 1, :]
        o_ref[:, hd * DV:(hd + 1) * DV] = o.T.astype(BF16)


def _mla_attention(qm, km, vmt, n_q_tiles):
    nb = qm.shape[0]
    resident = lambda shape: pl.BlockSpec((None,) + shape, lambda b, i: (b,) + (0,) * len(shape),
                                          pipeline_mode=pl.Buffered(1))
    return pl.pallas_call(
        _mla_attn_kernel,
        out_shape=jax.ShapeDtypeStruct((nb, n_q_tiles * T, HEADS * DV), BF16),
        grid=(nb, n_q_tiles),
        in_specs=[pl.BlockSpec((None, T, HEADS * 256), lambda b, i: (b, i, 0)),
                  resident((ST, HEADS * 256)), resident((NT, HEADS * MLA_VROWS, T))],
        out_specs=pl.BlockSpec((None, T, HEADS * DV), lambda b, i: (b, i, 0)),
        scratch_shapes=[pltpu.VMEM((HEADS, T, T), F32),
                        pltpu.VMEM((HEADS, 1, T), F32), pltpu.VMEM((HEADS, MLA_VROWS, T), F32)],
        compiler_params=_params(2),
        name="mla_attention",
    )(qm, km, vmt)


def _diff_attn_kernel(lam_init, q_ref, k_ref, vt_ref, lqk_ref, g_ref, o_ref, qm_ref, s_ref, m_ref, acc_ref):
    q = q_ref[...]
    lane = lax.broadcasted_iota(jnp.int32, (T, 128), 1)
    zero = jnp.zeros_like(q)
    for j in range(4):
        qm_ref[j] = jnp.where(lane // DIFF_DQK == j, q, zero)

    def qk(j, c):
        return lax.dot_general(k_ref[_key_rows(c), :], qm_ref[j], _NT_DIMS, preferred_element_type=F32)

    _attend(4, qk, lambda j, c: vt_ref[c, (j // 2) * DIFF_VROWS:(j // 2 + 1) * DIFF_VROWS, :],
            s_ref, m_ref, acc_ref, pl.program_id(2) == NT_LAT)

    lqk = lqk_ref[...]
    lam = (jnp.exp(jnp.sum(lqk[0:1] * lqk[1:2], axis=-1, keepdims=True))
           - jnp.exp(jnp.sum(lqk[2:3] * lqk[3:4], axis=-1, keepdims=True)) + lam_init)
    attn = lambda j: acc_ref[j, 0:DIFF_DV, :] / acc_ref[j, DIFF_DV:DIFF_DV + 1, :]
    heads = []
    for hh in range(2):
        d = attn(2 * hh) - lam * attn(2 * hh + 1)
        ms = jnp.mean(d * d, axis=0, keepdims=True)
        heads.append(d * lax.rsqrt(ms + EPS))
    out = jnp.concatenate(heads, axis=0).T
    o_ref[...] = ((out * g_ref[...]) * (1.0 - lam_init)).astype(BF16)


def _diff_attention(qd, kd, vdt, lqk, g2, lam_init, n_q_tiles):
    nb = qd.shape[0]
    return pl.pallas_call(
        functools.partial(_diff_attn_kernel, lam_init),
        out_shape=jax.ShapeDtypeStruct((nb, n_q_tiles * T, DIFF_W), BF16),
        grid=(nb, 2, n_q_tiles),
        in_specs=[pl.BlockSpec((None, T, 128), lambda b, h, i: (b, i, h)),
                  pl.BlockSpec((None, ST, 128), lambda b, h, i: (b, 0, h)),
                  pl.BlockSpec((None, NT, 2 * DIFF_VROWS, T), lambda b, h, i: (b, 0, h, 0)),
                  pl.BlockSpec((4, DIFF_DQK), lambda b, h, i: (0, 0)),
                  pl.BlockSpec((1, 128), lambda b, h, i: (0, 0))],
        out_specs=pl.BlockSpec((None, T, 128), lambda b, h, i: (b, i, h)),
        scratch_shapes=[pltpu.VMEM((4, T, 128), BF16), pltpu.VMEM((4, T, T), F32),
                        pltpu.VMEM((4, 1, T), F32), pltpu.VMEM((4, DIFF_VROWS, T), F32)],
        compiler_params=_params(3),
        name="diff_attention",
    )(qd, kd, vdt, lqk, g2)


def _outproj_kernel(x_ref, om_ref, od_ref, cm_ref, mod_ref, wo_ref, g_ref, b_ref, o_ref):
    r = _mod_row(pl.program_id(1))
    g1 = mod_ref[pl.ds(r, 1), 2 * D:3 * D]
    y = (_dot(om_ref[...], wo_ref[0:512, :]) + _dot(od_ref[...], wo_ref[512:768, :])
         + _dot(cm_ref[...], wo_ref[768:1024, :]))
    o_ref[...] = _ln(DN_ALPHA * x_ref[...] + g1 * y, g_ref[...], b_ref[...])


def _outproj(xs, om, od, cm, mod, w, n_tiles):
    nb = xs.shape[0]
    row = lambda width: pl.BlockSpec((None, T, width), lambda b, i: (b, i, 0))
    return pl.pallas_call(
        _outproj_kernel,
        out_shape=jax.ShapeDtypeStruct((nb, n_tiles * T, D), F32),
        grid=(nb, n_tiles),
        in_specs=[row(D), row(512), row(256), row(256), _const_spec((MOD_ROWS, 6 * D)),
                  _const_spec((D, D)), _const_spec((1, D)), _const_spec((1, D))],
        out_specs=row(D),
        compiler_params=_params(2),
        name="outproj",
    )(xs, om, od, cm, mod, w["wo"], w["ln1g"], w["ln1b"])


def _ffn_kernel(x_ref, xp_ref, xn_ref, mod_ref, wup_ref, cw_ref, cb_ref, wdn_ref, g_ref, b_ref,
                o_ref, h_ref, acc_ref):
    i = pl.program_id(1)
    r = _mod_row(i)
    sh2 = mod_ref[pl.ds(r, 1), 3 * D:4 * D]
    sc2 = mod_ref[pl.ds(r, 1), 4 * D:5 * D]
    g2 = mod_ref[pl.ds(r, 1), 5 * D:6 * D]
    x = x_ref[...]
    modulate = lambda t: t * (1.0 + sc2) + sh2
    seq_start = jnp.logical_or(i == 0, i == NT_LAT)
    seq_end = i >= NT_LAT - 1
    h_ref[0:HALO, :] = jnp.where(seq_start, 0.0, modulate(xp_ref[...])).astype(BF16)
    hm = modulate(x)
    row = lax.broadcasted_iota(jnp.int32, (T, 1), 0)
    hm = jnp.where(jnp.logical_and(i == NT_LAT, row >= C), 0.0, hm)
    h_ref[HALO:HALO + T, :] = hm.astype(BF16)
    h_ref[HALO + T:, :] = jnp.where(seq_end, 0.0, modulate(xn_ref[...])).astype(BF16)
    h = h_ref[...]

    def conv(col):
        up = _dot(h, wup_ref[:, col:col + FF_CHUNK])
        cw = cw_ref[:, col:col + FF_CHUNK]
        prev = pltpu.roll(up, 1, 0)[HALO:HALO + T]
        nxt = pltpu.roll(up, T + 2 * HALO - 1, 0)[HALO:HALO + T]
        return prev * cw[0:1] + up[HALO:HALO + T] * cw[1:2] + nxt * cw[2:3] + cb_ref[:, col:col + FF_CHUNK]

    for j in range(D_FF // FF_CHUNK):
        gate = conv(j * FF_CHUNK)
        val = conv(D_FF + j * FF_CHUNK)
        act = ((gate * (1.0 / (1.0 + jnp.exp(-gate)))) * val).astype(BF16)
        part = _dot(act, wdn_ref[j * FF_CHUNK:(j + 1) * FF_CHUNK, :])
        if j == 0:
            acc_ref[...] = part
        else:
            acc_ref[...] += part
    o_ref[...] = _ln(DN_ALPHA * x + g2 * acc_ref[...], g_ref[...], b_ref[...])


def _ffn(x1, mod, w, n_tiles):
    nb, rows, _ = x1.shape
    nh = rows // HALO
    per = T // HALO
    return pl.pallas_call(
        _ffn_kernel,
        out_shape=jax.ShapeDtypeStruct((nb, n_tiles * T, D), F32),
        grid=(nb, n_tiles),
        in_specs=[pl.BlockSpec((None, T, D), lambda b, i: (b, i, 0)),
                  pl.BlockSpec((None, HALO, D), lambda b, i: (b, jnp.maximum(i * per - 1, 0), 0)),
                  pl.BlockSpec((None, HALO, D), lambda b, i: (b, jnp.minimum((i + 1) * per, nh - 1), 0)),
                  _const_spec((MOD_ROWS, 6 * D)), _const_spec((D, 2 * D_FF)),
                  _const_spec((3, 2 * D_FF)), _const_spec((1, 2 * D_FF)), _const_spec((D_FF, D)),
                  _const_spec((1, D)), _const_spec((1, D))],
        out_specs=pl.BlockSpec((None, T, D), lambda b, i: (b, i, 0)),
        scratch_shapes=[pltpu.VMEM((T + 2 * HALO, D), BF16), pltpu.VMEM((T, D), F32)],
        compiler_params=_params(2),
        name="conv_ffn",
    )(x1, x1, x1, mod, w["wup"], w["convw"], w["convb"], w["wdn"], w["ln2g"], w["ln2b"])


def _swap_halves(n_blocks, half):
    idx = jnp.arange(n_blocks * 2 * half)
    return jnp.where((idx // half) % 2 == 0, idx + half, idx - half)


def _rope_tables():
    t = jnp.arange(S, dtype=jnp.int32)
    rows = (t // GRID_W).astype(F32)[:, None]
    cols = (t % GRID_W).astype(F32)[:, None]

    def tables(n, width):
        inv = ROPE_BASE ** (-jnp.arange(n, dtype=F32) / n)
        ar, ac = rows * inv[None, :], cols * inv[None, :]
        cos = jnp.concatenate([jnp.cos(ar), jnp.cos(ar), jnp.cos(ac), jnp.cos(ac)], -1)
        sin = jnp.concatenate([-jnp.sin(ar), jnp.sin(ar), -jnp.sin(ac), jnp.sin(ac)], -1)
        reps = width // (4 * n)
        cos, sin = jnp.tile(cos, (1, reps)), jnp.tile(sin, (1, reps))
        cos = jnp.concatenate([cos, jnp.ones((ST - S, cos.shape[1]), F32)], 0)
        sin = jnp.concatenate([sin, jnp.zeros((ST - S, sin.shape[1]), F32)], 0)
        return cos, sin

    ckr, skr = tables(ROPE // 4, ROPE)
    pad = jnp.zeros((ST, 128 - ROPE), F32)
    ckr, skr = jnp.concatenate([ckr, pad], -1), jnp.concatenate([skr, pad], -1)
    cd, sd = tables(DIFF_DQK // 4, 128)
    return ckr, skr, cd, sd


def _layer_weights(l, tabs, w_in, mla_gq, mla_wuq, mla_gkv, mla_wukv, diff_lq1, diff_lk1, diff_lq2, diff_lk2,
                   diff_subln_g, sgu_ln_g, sgu_ln_b, sgu_ws, sgu_bs, w_o, ln1_g, ln1_b, ffn_wup, ffn_convw,
                   ffn_convb, ffn_wdown, ln2_g, ln2_b):
    win = w_in[l]
    k_r = win[:, 640:704]
    dq = win[:, 704:960]
    dk = win[:, 960:1216]
    zpad = jnp.zeros((D, 128 - ROPE), F32)
    p64 = _swap_halves(2, ROPE // 4)
    p256 = _swap_halves(16, DIFF_DQK // 4)
    win_wide = jnp.concatenate(
        [win[:, 0:640], k_r, zpad, dq, dk, win[:, 1216:1984],
         k_r[:, p64], zpad, dq[:, p256], dk[:, p256]], axis=1).astype(BF16)

    wuq = mla_wuq[l].reshape(Q_LORA, HEADS, NOPE + ROPE)
    zq = jnp.zeros((Q_LORA, HEADS, 128 - ROPE), F32)
    wuq_main = jnp.concatenate([wuq, zq], -1).reshape(Q_LORA, HEADS * 256).astype(BF16)
    wuq_part = jnp.concatenate([wuq[:, :, NOPE:][:, :, p64], zq], -1).reshape(Q_LORA, HEADS * 128).astype(BF16)
    wukv = mla_wukv[l].reshape(KV_LORA, HEADS, NOPE + DV)
    ckr, skr, cd, sd = tabs
    bias = jnp.repeat(sgu_bs[l].T, CHUNK_W // GROUPS, axis=1)
    return dict(
        win=win_wide, gq=mla_gq[l][None], wuq=wuq_main, wuqc=wuq_part, gkv=mla_gkv[l][None],
        wk=wukv[:, :, :NOPE].reshape(KV_LORA, HEADS * NOPE).astype(BF16),
        wv=wukv[:, :, NOPE:].reshape(KV_LORA, HEADS * DV).astype(BF16),
        ckr=ckr, skr=skr, cd=cd, sd=sd, lng=sgu_ln_g[l][None], lnb=sgu_ln_b[l][None],
        ws=sgu_ws[l].reshape(GROUPS * CHUNK, CHUNK).astype(BF16), bs=bias,
        lqk=jnp.stack([diff_lq1[l], diff_lk1[l], diff_lq2[l], diff_lk2[l]]),
        subg=jnp.tile(diff_subln_g[l], 2)[None],
        wo=w_o[l].astype(BF16), ln1g=ln1_g[l][None], ln1b=ln1_b[l][None],
        wup=ffn_wup[l].astype(BF16), convw=ffn_convw[l], convb=ffn_convb[l][None],
        wdn=ffn_wdown[l].astype(BF16), ln2g=ln2_g[l][None], ln2b=ln2_b[l][None])


def kernel(x, c, ctx, c_ctx, ada_w, ada_b, w_in, mla_gq, mla_wuq, mla_gkv, mla_wukv, diff_lq1, diff_lk1, diff_lq2, diff_lk2, diff_subln_g, sgu_ln_g, sgu_ln_b, sgu_ws, sgu_bs, w_o, ln1_g, ln1_b, ffn_wup, ffn_convw, ffn_convb, ffn_wdown, ln2_g, ln2_b):
    nb = x.shape[0]
    cvec = jnp.concatenate([c, c_ctx[None], jnp.zeros((MOD_ROWS - nb - 1, D), F32)], 0)
    mod = _modulation(cvec, ada_w, ada_b)
    tabs = _rope_tables()
    xs = jnp.concatenate([x, ctx, jnp.zeros((nb, ST - S - C, D), F32)], axis=1)
    for l in range(DEPTH):
        w = _layer_weights(l, tabs, w_in, mla_gq, mla_wuq, mla_gkv, mla_wukv, diff_lq1, diff_lk1, diff_lq2,
                           diff_lk2, diff_subln_g, sgu_ln_g, sgu_ln_b, sgu_ws, sgu_bs, w_o, ln1_g, ln1_b,
                           ffn_wup, ffn_convw, ffn_convb, ffn_wdown, ln2_g, ln2_b)
        n_tiles = NT if l < DEPTH - 1 else NT_LAT
        lam_init = 0.8 - 0.6 * math.exp(-0.3 * l)
        qm, km, vmt, qd, kd, vdt, cm = _inproj(xs, mod[l], w)
        om = _mla_attention(qm, km, vmt, n_tiles)
        od = _diff_attention(qd, kd, vdt, w["lqk"], w["subg"], lam_init, n_tiles)
        x1 = _outproj(xs, om, od, cm, mod[l], w, n_tiles)
        xs = _ffn(x1, mod[l], w, n_tiles)
    return xs
```

```python
import functools
import math

import jax
import jax.numpy as jnp
import numpy as np
from jax import lax
from jax.experimental import pallas as pl
from jax.experimental.pallas import tpu as pltpu

F32 = jnp.float32
BF16 = jnp.bfloat16

D = 1024
S = 8192
C = 256
GRID_W = 64
DEPTH = 2
T = 512
ST = S + 2 * C
NT = ST // T
NT_LAT = S // T
HEADS = 4
Q_LORA = 384
KV_LORA = 256
NOPE = 128
ROPE = 64
DV = 128
DIFF_W = 256
DIFF_DQK = 32
DIFF_DV = 64
CHUNK_W = 256
CHUNK = 128
GROUPS = 4
D_FF = 2816
FF_CHUNK = 256
HALO = 16
ONES_ROWS = 16
MLA_VROWS = DV + ONES_ROWS
DIFF_VROWS = DIFF_DV + ONES_ROWS
EPS = 1e-6
ROPE_BASE = 10000.0
DN_ALPHA = (2 * DEPTH) ** 0.25
MLA_SCALE = (NOPE + ROPE) ** -0.5
DIFF_SCALE = DIFF_DQK ** -0.5
LOG2E = math.log2(math.e)
MLA_QSCALE = MLA_SCALE * LOG2E
DIFF_QSCALE = DIFF_SCALE * LOG2E
ZW = 2688
MOD_ROWS = 8
CTX_ROW = 2
VMEM_LIMIT = 56 * 1024 * 1024

_NT_DIMS = (((1,), (1,)), ((), ()))


def _dot(a, b):
    return jnp.dot(a, b, preferred_element_type=F32)


def _rms(x, g):
    return (x * lax.rsqrt(jnp.mean(x * x, axis=-1, keepdims=True) + EPS)) * g


def _ln(x, g, b):
    mu = jnp.mean(x, axis=-1, keepdims=True)
    xc = x - mu
    var = jnp.mean(xc * xc, axis=-1, keepdims=True)
    return (xc * lax.rsqrt(var + EPS)) * g + b


def _params(n_axes):
    return pltpu.CompilerParams(dimension_semantics=("arbitrary",) * n_axes,
                                vmem_limit_bytes=VMEM_LIMIT)


def _const_spec(shape):
    nd = len(shape)
    return pl.BlockSpec(shape, lambda *_: (0,) * nd, pipeline_mode=pl.Buffered(1))


MOD_BLOCK = 1536


def _mod_kernel(c_ref, w_ref, b_ref, o_ref):
    c = c_ref[...]
    sc = (c * (1.0 / (1.0 + jnp.exp(-c)))).astype(BF16)
    o_ref[...] = _dot(sc, w_ref[...].astype(BF16)) + b_ref[...]


def _modulation(cvec, ada_w, ada_b):
    return pl.pallas_call(
        _mod_kernel,
        out_shape=jax.ShapeDtypeStruct((DEPTH, MOD_ROWS, 6 * D), F32),
        grid=(DEPTH, 6 * D // MOD_BLOCK),
        in_specs=[pl.BlockSpec((MOD_ROWS, D), lambda l, j: (0, 0)),
                  pl.BlockSpec((None, D, MOD_BLOCK), lambda l, j: (l, 0, j)),
                  pl.BlockSpec((None, 1, MOD_BLOCK), lambda l, j: (l, 0, j))],
        out_specs=pl.BlockSpec((None, MOD_ROWS, MOD_BLOCK), lambda l, j: (l, 0, j)),
        compiler_params=_params(2),
        name="modulation",
    )(cvec, ada_w, ada_b.reshape(DEPTH, 1, 6 * D))


def _stream_tile(x_ref, c_ref, tile):
    context = jnp.concatenate([c_ref[...], jnp.broadcast_to(c_ref[C - 1:C, :], (T - C, D))], axis=0)
    return jnp.where(tile == NT_LAT, context, x_ref[...])


def _stream_specs(x, c):
    ctx_block = 0 if c.shape[1] == C else S // C
    return [pl.BlockSpec((None, T, D), lambda b, i: (b, jnp.minimum(i, NT_LAT - 1), 0)),
            pl.BlockSpec((None, C, D), lambda b, i: (b, ctx_block, 0))]


def _mod_row(tile):
    return jnp.where(tile == NT_LAT, CTX_ROW, pl.program_id(0))


def _inproj_kernel(x_ref, c_ref, mod_ref, win_ref, gq_ref, wuq_ref, wuqc_ref, gkv_ref, wk_ref, wv_ref,
                   ckr_ref, skr_ref, cd_ref, sd_ref, lng_ref, lnb_ref, ws_ref, bs_ref,
                   qmt_ref, km_ref, vmt_ref, qdt_ref, kd_ref, vdt_ref, cm_ref):
    r = _mod_row(pl.program_id(1))
    sh1 = mod_ref[pl.ds(r, 1), 0:D]
    sc1 = mod_ref[pl.ds(r, 1), D:2 * D]
    h = (_stream_tile(x_ref, c_ref, pl.program_id(1)) * (1.0 + sc1) + sh1).astype(BF16)
    z = _dot(h, win_ref[...])

    ckr = ckr_ref[...]
    skr = skr_ref[...]
    cd = cd_ref[...]
    sd = sd_ref[...]

    qn = _rms(z[:, 0:Q_LORA], gq_ref[...]).astype(BF16)
    qa = _dot(qn, wuq_ref[...])
    qc = _dot(qn, wuqc_ref[...])
    for hd in range(HEADS):
        nope = qa[:, hd * 256:hd * 256 + NOPE]
        rot = qa[:, hd * 256 + NOPE:(hd + 1) * 256] * ckr + qc[:, hd * 128:(hd + 1) * 128] * skr
        qmt_ref[hd * 256:hd * 256 + NOPE, :] = (nope * MLA_QSCALE).T.astype(BF16)
        qmt_ref[hd * 256 + NOPE:(hd + 1) * 256, :] = (rot * MLA_QSCALE).T.astype(BF16)

    kvn = _rms(z[:, Q_LORA:Q_LORA + KV_LORA], gkv_ref[...]).astype(BF16)
    kn = _dot(kvn, wk_ref[...])
    ones = jnp.ones((ONES_ROWS, T), BF16)
    vt = _dot(kvn, wv_ref[...]).T.astype(BF16)
    for hd in range(HEADS):
        vmt_ref[hd * MLA_VROWS:hd * MLA_VROWS + DV, :] = vt[hd * DV:(hd + 1) * DV]
        vmt_ref[hd * MLA_VROWS + DV:(hd + 1) * MLA_VROWS, :] = ones
    kr = (z[:, 640:768] * ckr + z[:, 2048:2176] * skr).astype(BF16)
    for hd in range(HEADS):
        km_ref[:, hd * 256:hd * 256 + NOPE] = kn[:, hd * 128:(hd + 1) * 128].astype(BF16)
        km_ref[:, hd * 256 + NOPE:(hd + 1) * 256] = kr

    for blk in range(2):
        lo = blk * 128
        qdt_ref[lo:lo + 128, :] = ((z[:, 768 + lo:896 + lo] * cd + z[:, 2176 + lo:2304 + lo] * sd)
                                   * DIFF_QSCALE).T.astype(BF16)
        kd_ref[:, lo:lo + 128] = (z[:, 1024 + lo:1152 + lo] * cd
                                  + z[:, 2432 + lo:2560 + lo] * sd).astype(BF16)
    vt = z[:, 1280:1536].T.astype(BF16)
    for hd in range(HEADS):
        vdt_ref[hd * DIFF_VROWS:hd * DIFF_VROWS + DIFF_DV, :] = vt[hd * DIFF_DV:(hd + 1) * DIFF_DV]
        vdt_ref[hd * DIFF_VROWS + DIFF_DV:(hd + 1) * DIFF_VROWS, :] = ones

    zc = z[:, 1536:2048]
    zc = 0.5 * zc * (1.0 + jnp.tanh(math.sqrt(2.0 / math.pi) * (zc + 0.044715 * (zc * zc * zc))))
    u = zc[:, :CHUNK_W]
    v = _ln(zc[:, CHUNK_W:], lng_ref[...], lnb_ref[...]).astype(BF16)
    group = lax.broadcasted_iota(jnp.int32, (CHUNK, CHUNK_W), 1) // (CHUNK_W // GROUPS)
    ws = ws_ref[...]
    for n in range(T // CHUNK):
        mixed_all = _dot(ws, v[n * CHUNK:(n + 1) * CHUNK, :])
        mixed = mixed_all[0:CHUNK]
        for g in range(1, GROUPS):
            mixed = jnp.where(group == g, mixed_all[g * CHUNK:(g + 1) * CHUNK], mixed)
        cm_ref[n * CHUNK:(n + 1) * CHUNK, :] = (u[n * CHUNK:(n + 1) * CHUNK, :]
                                                * (mixed + bs_ref[...])).astype(BF16)


def _inproj(x, c, mod, w):
    nb = x.shape[0]
    row = lambda width: pl.BlockSpec((None, T, width), lambda b, i: (b, i, 0))
    tab = pl.BlockSpec((T, 128), lambda b, i: (i, 0))
    rows = lambda width: jax.ShapeDtypeStruct((nb, ST, width), BF16)
    cols = lambda width: jax.ShapeDtypeStruct((nb, NT, width, T), BF16)
    col = lambda width: pl.BlockSpec((None, None, width, T), lambda b, i: (b, i, 0, 0))
    outs = [cols(HEADS * 256), rows(1024), cols(HEADS * MLA_VROWS), cols(DIFF_W), rows(256),
            cols(HEADS * DIFF_VROWS), rows(256)]
    return pl.pallas_call(
        _inproj_kernel,
        out_shape=outs,
        grid=(nb, NT),
        in_specs=_stream_specs(x, c) + [
                  _const_spec((MOD_ROWS, 6 * D)), _const_spec((D, ZW)),
                  _const_spec((1, Q_LORA)), _const_spec((Q_LORA, HEADS * 256)), _const_spec((Q_LORA, HEADS * 128)),
                  _const_spec((1, KV_LORA)), _const_spec((KV_LORA, HEADS * NOPE)), _const_spec((KV_LORA, HEADS * DV)),
                  tab, tab, tab, tab,
                  _const_spec((1, CHUNK_W)), _const_spec((1, CHUNK_W)),
                  _const_spec((GROUPS * CHUNK, CHUNK)), _const_spec((CHUNK, CHUNK_W))],
        out_specs=[col(HEADS * 256), row(1024), col(HEADS * MLA_VROWS), col(DIFF_W), row(256),
                   col(HEADS * DIFF_VROWS), row(256)],
        compiler_params=_params(2),
        name="inproj",
    )(x, c, mod, w["win"], w["gq"], w["wuq"], w["wuqc"], w["gkv"], w["wk"], w["wv"],
      w["ckr"], w["skr"], w["cd"], w["sd"], w["lng"], w["lnb"], w["ws"], w["bs"])


def _key_rows(c):
    if isinstance(c, int):
        return pl.ds(c * T, T)
    return pl.ds(pl.multiple_of(c * T, T), T)


def _attend(n, qk, vt, s_ref, cmax_ref, m_ref, acc_ref, only_context):
    m_ref[...] = jnp.full(m_ref.shape, -jnp.inf, F32)
    acc_ref[...] = jnp.zeros(acc_ref.shape, F32)

    def scores(j, c):
        s = qk(j, c)
        s_ref[j] = s
        cmax_ref[j] = jnp.max(s, axis=0, keepdims=True)

    def update(j, c, mask_padding=False):
        s = s_ref[j]
        if mask_padding:
            key = lax.broadcasted_iota(jnp.int32, (T, T), 0)
            s = jnp.where(key < C, s, -jnp.inf)
        m_prev = m_ref[j]
        m_new = jnp.maximum(m_prev, cmax_ref[j])
        alpha = jnp.exp2(m_prev - m_new)
        p = jnp.exp2(s - m_new).astype(BF16)
        acc_ref[j] = alpha * acc_ref[j] + _dot(vt(j, c), p)
        m_ref[j] = m_new

    @pl.when(jnp.logical_not(only_context))
    def _():
        for j in range(n - 1):
            scores(j, 0)

        def body(c, carry):
            scores(n - 1, c)
            for j in range(n - 1):
                update(j, c)
                scores(j, c + 1)
            update(n - 1, c)
            return carry

        lax.fori_loop(0, NT_LAT, body, 0)

    @pl.when(only_context)
    def _():
        for j in range(n - 1):
            scores(j, NT_LAT)

    scores(n - 1, NT_LAT)
    for j in range(n):
        update(j, NT_LAT, mask_padding=True)


def _mla_attn_kernel(qt_ref, k_ref, vt_ref, o_ref, s_ref, cmax_ref, m_ref, acc_ref):
    def qk(j, c):
        return _dot(k_ref[_key_rows(c), j * 256:(j + 1) * 256], qt_ref[j * 256:(j + 1) * 256, :])

    _attend(HEADS, qk, lambda j, c: vt_ref[c, j * MLA_VROWS:(j + 1) * MLA_VROWS, :],
            s_ref, cmax_ref, m_ref, acc_ref, pl.program_id(1) == NT_LAT)
    for hd in range(HEADS):
        o = acc_ref[hd, 0:DV, :] / acc_ref[hd, DV:DV + 1, :]
        o_ref[:, hd * DV:(hd + 1) * DV] = o.T.astype(BF16)


def _mla_attention(qm, km, vmt, n_q_tiles):
    nb = qm.shape[0]
    resident = lambda shape: pl.BlockSpec((None,) + shape, lambda b, i: (b,) + (0,) * len(shape),
                                          pipeline_mode=pl.Buffered(1))
    return pl.pallas_call(
        _mla_attn_kernel,
        out_shape=jax.ShapeDtypeStruct((nb, n_q_tiles * T, HEADS * DV), BF16),
        grid=(nb, n_q_tiles),
        in_specs=[pl.BlockSpec((None, None, HEADS * 256, T), lambda b, i: (b, i, 0, 0)),
                  resident((ST, HEADS * 256)), resident((NT, HEADS * MLA_VROWS, T))],
        out_specs=pl.BlockSpec((None, T, HEADS * DV), lambda b, i: (b, i, 0)),
        scratch_shapes=[pltpu.VMEM((HEADS, T, T), F32),
                        pltpu.VMEM((HEADS, 1, T), F32), pltpu.VMEM((HEADS, 1, T), F32),
                        pltpu.VMEM((HEADS, MLA_VROWS, T), F32)],
        compiler_params=_params(2),
        name="mla_attention",
    )(qm, km, vmt)


DIFF_MAPS = 2 * HEADS


MAPS_PER_BLOCK = 128 // DIFF_DQK


def _diff_attn_kernel(lam_init, qt_ref, k_ref, vt_ref, lqk_ref, g_ref, o_ref, qm_ref, s_ref, cmax_ref, m_ref, acc_ref):
    dim = lax.broadcasted_iota(jnp.int32, (128, T), 0)
    for j in range(DIFF_MAPS):
        blk = j // MAPS_PER_BLOCK
        qt = qt_ref[blk * 128:(blk + 1) * 128, :]
        qm_ref[j] = jnp.where(dim // DIFF_DQK == j % MAPS_PER_BLOCK, qt, jnp.zeros_like(qt))

    def qk(j, c):
        blk = j // MAPS_PER_BLOCK
        return _dot(k_ref[_key_rows(c), blk * 128:(blk + 1) * 128], qm_ref[j])

    _attend(DIFF_MAPS, qk, lambda j, c: vt_ref[c, (j // 2) * DIFF_VROWS:(j // 2 + 1) * DIFF_VROWS, :],
            s_ref, cmax_ref, m_ref, acc_ref, pl.program_id(1) == NT_LAT)

    lqk = lqk_ref[...]
    lam = (jnp.exp(jnp.sum(lqk[0:1] * lqk[1:2], axis=-1, keepdims=True))
           - jnp.exp(jnp.sum(lqk[2:3] * lqk[3:4], axis=-1, keepdims=True)) + lam_init)
    attn = lambda j: acc_ref[j, 0:DIFF_DV, :] / acc_ref[j, DIFF_DV:DIFF_DV + 1, :]
    heads = []
    for hd in range(HEADS):
        d = attn(2 * hd) - lam * attn(2 * hd + 1)
        ms = jnp.mean(d * d, axis=0, keepdims=True)
        heads.append(d * lax.rsqrt(ms + EPS))
    out = jnp.concatenate(heads, axis=0).T
    o_ref[...] = ((out * g_ref[...]) * (1.0 - lam_init)).astype(BF16)


def _diff_attention(qd, kd, vdt, lqk, g_row, lam_init, n_q_tiles):
    nb = qd.shape[0]
    return pl.pallas_call(
        functools.partial(_diff_attn_kernel, lam_init),
        out_shape=jax.ShapeDtypeStruct((nb, n_q_tiles * T, DIFF_W), BF16),
        grid=(nb, n_q_tiles),
        in_specs=[pl.BlockSpec((None, None, DIFF_W, T), lambda b, i: (b, i, 0, 0)),
                  pl.BlockSpec((None, ST, DIFF_W), lambda b, i: (b, 0, 0)),
                  pl.BlockSpec((None, NT, HEADS * DIFF_VROWS, T), lambda b, i: (b, 0, 0, 0)),
                  pl.BlockSpec((4, DIFF_DQK), lambda b, i: (0, 0)),
                  pl.BlockSpec((1, DIFF_W), lambda b, i: (0, 0))],
        out_specs=pl.BlockSpec((None, T, DIFF_W), lambda b, i: (b, i, 0)),
        scratch_shapes=[pltpu.VMEM((DIFF_MAPS, 128, T), BF16), pltpu.VMEM((DIFF_MAPS, T, T), F32),
                        pltpu.VMEM((DIFF_MAPS, 1, T), F32), pltpu.VMEM((DIFF_MAPS, 1, T), F32),
                        pltpu.VMEM((DIFF_MAPS, DIFF_VROWS, T), F32)],
        compiler_params=_params(2),
        name="diff_attention",
    )(qd, kd, vdt, lqk, g_row)


def _outproj_kernel(x_ref, c_ref, om_ref, od_ref, cm_ref, mod_ref, wo_ref, g_ref, b_ref, o_ref):
    r = _mod_row(pl.program_id(1))
    g1 = mod_ref[pl.ds(r, 1), 2 * D:3 * D]
    y = (_dot(om_ref[...], wo_ref[0:512, :]) + _dot(od_ref[...], wo_ref[512:768, :])
         + _dot(cm_ref[...], wo_ref[768:1024, :]))
    x = _stream_tile(x_ref, c_ref, pl.program_id(1))
    o_ref[...] = _ln(DN_ALPHA * x + g1 * y, g_ref[...], b_ref[...])


def _outproj(x, c, om, od, cm, mod, w, n_tiles):
    nb = x.shape[0]
    row = lambda width: pl.BlockSpec((None, T, width), lambda b, i: (b, i, 0))
    return pl.pallas_call(
        _outproj_kernel,
        out_shape=jax.ShapeDtypeStruct((nb, n_tiles * T, D), F32),
        grid=(nb, n_tiles),
        in_specs=_stream_specs(x, c) + [
                  row(512), row(256), row(256), _const_spec((MOD_ROWS, 6 * D)),
                  _const_spec((D, D)), _const_spec((1, D)), _const_spec((1, D))],
        out_specs=row(D),
        compiler_params=_params(2),
        name="outproj",
    )(x, c, om, od, cm, mod, w["wo"], w["ln1g"], w["ln1b"])


def _ffn_kernel(x_ref, xp_ref, xn_ref, mod_ref, wup_ref, cw_ref, cb_ref, wdn_ref, g_ref, b_ref,
                o_ref, h_ref, act_ref):
    i = pl.program_id(1)
    r = _mod_row(i)
    sh2 = mod_ref[pl.ds(r, 1), 3 * D:4 * D]
    sc2 = mod_ref[pl.ds(r, 1), 4 * D:5 * D]
    g2 = mod_ref[pl.ds(r, 1), 5 * D:6 * D]
    x = x_ref[...]
    modulate = lambda t: t * (1.0 + sc2) + sh2
    seq_start = jnp.logical_or(i == 0, i == NT_LAT)
    seq_end = i >= NT_LAT - 1
    h_ref[0:HALO, :] = jnp.where(seq_start, 0.0, modulate(xp_ref[...])).astype(BF16)
    hm = modulate(x)
    row = lax.broadcasted_iota(jnp.int32, (T, 1), 0)
    hm = jnp.where(jnp.logical_and(i == NT_LAT, row >= C), 0.0, hm)
    h_ref[HALO:HALO + T, :] = hm.astype(BF16)
    h_ref[HALO + T:, :] = jnp.where(seq_end, 0.0, modulate(xn_ref[...])).astype(BF16)
    h = h_ref[...]

    def conv(col):
        up = _dot(h, wup_ref[:, col:col + FF_CHUNK])
        cw = cw_ref[:, col:col + FF_CHUNK]
        prev = pltpu.roll(up, 1, 0)[HALO:HALO + T]
        nxt = pltpu.roll(up, T + 2 * HALO - 1, 0)[HALO:HALO + T]
        return prev * cw[0:1] + up[HALO:HALO + T] * cw[1:2] + nxt * cw[2:3] + cb_ref[:, col:col + FF_CHUNK]

    for j in range(D_FF // FF_CHUNK):
        gate = conv(j * FF_CHUNK)
        val = conv(D_FF + j * FF_CHUNK)
        act_ref[:, j * FF_CHUNK:(j + 1) * FF_CHUNK] = ((gate * (1.0 / (1.0 + jnp.exp(-gate)))) * val).astype(BF16)
    y = _dot(act_ref[...], wdn_ref[...])
    o_ref[...] = _ln(DN_ALPHA * x + g2 * y, g_ref[...], b_ref[...])


def _ffn(x1, mod, w, n_tiles):
    nb, rows, _ = x1.shape
    nh = rows // HALO
    per = T // HALO
    return pl.pallas_call(
        _ffn_kernel,
        out_shape=jax.ShapeDtypeStruct((nb, n_tiles * T, D), F32),
        grid=(nb, n_tiles),
        in_specs=[pl.BlockSpec((None, T, D), lambda b, i: (b, i, 0)),
                  pl.BlockSpec((None, HALO, D), lambda b, i: (b, jnp.maximum(i * per - 1, 0), 0)),
                  pl.BlockSpec((None, HALO, D), lambda b, i: (b, jnp.minimum((i + 1) * per, nh - 1), 0)),
                  _const_spec((MOD_ROWS, 6 * D)), _const_spec((D, 2 * D_FF)),
                  _const_spec((3, 2 * D_FF)), _const_spec((1, 2 * D_FF)), _const_spec((D_FF, D)),
                  _const_spec((1, D)), _const_spec((1, D))],
        out_specs=pl.BlockSpec((None, T, D), lambda b, i: (b, i, 0)),
        scratch_shapes=[pltpu.VMEM((T + 2 * HALO, D), BF16), pltpu.VMEM((T, D_FF), BF16)],
        compiler_params=_params(2),
        name="conv_ffn",
    )(x1, x1, x1, mod, w["wup"], w["convw"], w["convb"], w["wdn"], w["ln2g"], w["ln2b"])


def _swap_halves(n_blocks, half):
    idx = jnp.arange(n_blocks * 2 * half)
    return jnp.where((idx // half) % 2 == 0, idx + half, idx - half)


def _rope_tables():
    grid_h = S // GRID_W

    def tables(n, width):
        inv = ROPE_BASE ** (-np.arange(n, dtype=np.float64) / n)
        ar = np.arange(grid_h, dtype=np.float64)[:, None] * inv[None, :]
        ac = np.arange(GRID_W, dtype=np.float64)[:, None] * inv[None, :]
        per_row = lambda a, b: jnp.repeat(jnp.asarray(np.concatenate([a, b], -1), F32), GRID_W, axis=0)
        per_col = lambda a, b: jnp.tile(jnp.asarray(np.concatenate([a, b], -1), F32), (grid_h, 1))
        cos = jnp.concatenate([per_row(np.cos(ar), np.cos(ar)), per_col(np.cos(ac), np.cos(ac))], -1)
        sin = jnp.concatenate([per_row(-np.sin(ar), np.sin(ar)), per_col(-np.sin(ac), np.sin(ac))], -1)
        reps = width // (4 * n)
        cos, sin = jnp.tile(cos, (1, reps)), jnp.tile(sin, (1, reps))
        cos = jnp.concatenate([cos, jnp.ones((ST - S, cos.shape[1]), F32)], 0)
        sin = jnp.concatenate([sin, jnp.zeros((ST - S, sin.shape[1]), F32)], 0)
        return cos, sin

    ckr, skr = tables(ROPE // 4, ROPE)
    pad = jnp.zeros((ST, 128 - ROPE), F32)
    ckr, skr = jnp.concatenate([ckr, pad], -1), jnp.concatenate([skr, pad], -1)
    cd, sd = tables(DIFF_DQK // 4, 128)
    return ckr, skr, cd, sd


def _layer_weights(l, tabs, w_in, mla_gq, mla_wuq, mla_gkv, mla_wukv, diff_lq1, diff_lk1, diff_lq2, diff_lk2,
                   diff_subln_g, sgu_ln_g, sgu_ln_b, sgu_ws, sgu_bs, w_o, ln1_g, ln1_b, ffn_wup, ffn_convw,
                   ffn_convb, ffn_wdown, ln2_g, ln2_b):
    win = w_in[l]
    k_r = win[:, 640:704]
    dq = win[:, 704:960]
    dk = win[:, 960:1216]
    zpad = jnp.zeros((D, 128 - ROPE), F32)
    p64 = _swap_halves(2, ROPE // 4)
    p256 = _swap_halves(16, DIFF_DQK // 4)
    win_wide = jnp.concatenate(
        [win[:, 0:640], k_r, zpad, dq, dk, win[:, 1216:1984],
         k_r[:, p64], zpad, dq[:, p256], dk[:, p256]], axis=1).astype(BF16)

    wuq = mla_wuq[l].reshape(Q_LORA, HEADS, NOPE + ROPE)
    zq = jnp.zeros((Q_LORA, HEADS, 128 - ROPE), F32)
    wuq_main = jnp.concatenate([wuq, zq], -1).reshape(Q_LORA, HEADS * 256).astype(BF16)
    wuq_part = jnp.concatenate([wuq[:, :, NOPE:][:, :, p64], zq], -1).reshape(Q_LORA, HEADS * 128).astype(BF16)
    wukv = mla_wukv[l].reshape(KV_LORA, HEADS, NOPE + DV)
    ckr, skr, cd, sd = tabs
    bias = jnp.repeat(sgu_bs[l].T, CHUNK_W // GROUPS, axis=1)
    return dict(
        win=win_wide, gq=mla_gq[l][None], wuq=wuq_main, wuqc=wuq_part, gkv=mla_gkv[l][None],
        wk=wukv[:, :, :NOPE].reshape(KV_LORA, HEADS * NOPE).astype(BF16),
        wv=wukv[:, :, NOPE:].reshape(KV_LORA, HEADS * DV).astype(BF16),
        ckr=ckr, skr=skr, cd=cd, sd=sd, lng=sgu_ln_g[l][None], lnb=sgu_ln_b[l][None],
        ws=sgu_ws[l].reshape(GROUPS * CHUNK, CHUNK).astype(BF16), bs=bias,
        lqk=jnp.stack([diff_lq1[l], diff_lk1[l], diff_lq2[l], diff_lk2[l]]),
        subg=jnp.tile(diff_subln_g[l], HEADS)[None],
        wo=w_o[l].astype(BF16), ln1g=ln1_g[l][None], ln1b=ln1_b[l][None],
        wup=ffn_wup[l].astype(BF16), convw=ffn_convw[l], convb=ffn_convb[l][None],
        wdn=ffn_wdown[l].astype(BF16), ln2g=ln2_g[l][None], ln2b=ln2_b[l][None])


def kernel(x, c, ctx, c_ctx, ada_w, ada_b, w_in, mla_gq, mla_wuq, mla_gkv, mla_wukv, diff_lq1, diff_lk1, diff_lq2, diff_lk2, diff_subln_g, sgu_ln_g, sgu_ln_b, sgu_ws, sgu_bs, w_o, ln1_g, ln1_b, ffn_wup, ffn_convw, ffn_convb, ffn_wdown, ln2_g, ln2_b):
    nb = x.shape[0]
    cvec = jnp.concatenate([c, c_ctx[None], jnp.zeros((MOD_ROWS - nb - 1, D), F32)], 0)
    mod = _modulation(cvec, ada_w, ada_b)
    tabs = _rope_tables()
    xs, xc = x, ctx
    for l in range(DEPTH):
        w = _layer_weights(l, tabs, w_in, mla_gq, mla_wuq, mla_gkv, mla_wukv, diff_lq1, diff_lk1, diff_lq2,
                           diff_lk2, diff_subln_g, sgu_ln_g, sgu_ln_b, sgu_ws, sgu_bs, w_o, ln1_g, ln1_b,
                           ffn_wup, ffn_convw, ffn_convb, ffn_wdown, ln2_g, ln2_b)
        n_tiles = NT if l < DEPTH - 1 else NT_LAT
        lam_init = 0.8 - 0.6 * math.exp(-0.3 * l)
        qm, km, vmt, qd, kd, vdt, cm = _inproj(xs, xc, mod[l], w)
        om = _mla_attention(qm, km, vmt, n_tiles)
        od = _diff_attention(qd, kd, vdt, w["lqk"], w["subg"], lam_init, n_tiles)
        x1 = _outproj(xs, xc, om, od, cm, mod[l], w, n_tiles)
        xs = xc = _ffn(x1, mod[l], w, n_tiles)
    return xs
```

```python
import functools
import math

import jax
import jax.numpy as jnp
import numpy as np
from jax import lax
from jax.experimental import pallas as pl
from jax.experimental.pallas import tpu as pltpu

F32 = jnp.float32
BF16 = jnp.bfloat16

D = 1024
S = 8192
C = 256
GRID_W = 64
DEPTH = 2
T = 512
ST = S + 2 * C
NT = ST // T
NT_LAT = S // T
HEADS = 4
Q_LORA = 384
KV_LORA = 256
NOPE = 128
ROPE = 64
DV = 128
DIFF_W = 256
DIFF_DQK = 32
DIFF_DV = 64
CHUNK_W = 256
CHUNK = 128
GROUPS = 4
D_FF = 2816
FF_CHUNK = 256
HALO = 16
ONES_ROWS = 16
MLA_VROWS = DV + ONES_ROWS
DIFF_VROWS = DIFF_DV + ONES_ROWS
EPS = 1e-6
ROPE_BASE = 10000.0
DN_ALPHA = (2 * DEPTH) ** 0.25
MLA_SCALE = (NOPE + ROPE) ** -0.5
DIFF_SCALE = DIFF_DQK ** -0.5
LOG2E = math.log2(math.e)
MLA_QSCALE = MLA_SCALE * LOG2E
DIFF_QSCALE = DIFF_SCALE * LOG2E
ZW = 2688
MOD_ROWS = 8
CTX_ROW = 2
VMEM_LIMIT = 56 * 1024 * 1024

_NT_DIMS = (((1,), (1,)), ((), ()))


def _dot(a, b):
    return jnp.dot(a, b, preferred_element_type=F32)


def _rms(x, g):
    return (x * lax.rsqrt(jnp.mean(x * x, axis=-1, keepdims=True) + EPS)) * g


def _ln(x, g, b):
    mu = jnp.mean(x, axis=-1, keepdims=True)
    xc = x - mu
    var = jnp.mean(xc * xc, axis=-1, keepdims=True)
    return (xc * lax.rsqrt(var + EPS)) * g + b


def _params(n_axes):
    return pltpu.CompilerParams(dimension_semantics=("arbitrary",) * n_axes,
                                vmem_limit_bytes=VMEM_LIMIT)


def _layer_spec(l, shape):
    nd = len(shape)
    return pl.BlockSpec((None,) + shape, lambda *_: (l,) + (0,) * nd, pipeline_mode=pl.Buffered(1))


MOD_BLOCK = 1536


def _mod_kernel(c_ref, w_ref, b_ref, o_ref):
    c = c_ref[...]
    sc = (c * (1.0 / (1.0 + jnp.exp(-c)))).astype(BF16)
    o_ref[...] = _dot(sc, w_ref[...].astype(BF16)) + b_ref[...]


def _modulation(cvec, ada_w, ada_b):
    return pl.pallas_call(
        _mod_kernel,
        out_shape=jax.ShapeDtypeStruct((DEPTH, MOD_ROWS, 6 * D), F32),
        grid=(DEPTH, 6 * D // MOD_BLOCK),
        in_specs=[pl.BlockSpec((MOD_ROWS, D), lambda l, j: (0, 0)),
                  pl.BlockSpec((None, D, MOD_BLOCK), lambda l, j: (l, 0, j)),
                  pl.BlockSpec((None, 1, MOD_BLOCK), lambda l, j: (l, 0, j))],
        out_specs=pl.BlockSpec((None, MOD_ROWS, MOD_BLOCK), lambda l, j: (l, 0, j)),
        compiler_params=_params(2),
        name="modulation",
    )(cvec, ada_w, ada_b.reshape(DEPTH, 1, 6 * D))


def _stream_tile(x_ref, c_ref, tile):
    context = jnp.concatenate([c_ref[...], jnp.broadcast_to(c_ref[C - 1:C, :], (T - C, D))], axis=0)
    return jnp.where(tile == NT_LAT, context, x_ref[...])


def _stream_specs(x, c):
    ctx_block = 0 if c.shape[1] == C else S // C
    return [pl.BlockSpec((None, T, D), lambda b, i: (b, jnp.minimum(i, NT_LAT - 1), 0)),
            pl.BlockSpec((None, C, D), lambda b, i: (b, ctx_block, 0))]


def _mod_row(tile):
    return jnp.where(tile == NT_LAT, CTX_ROW, pl.program_id(0))


def _inproj_kernel(x_ref, c_ref, mod_ref, win_ref, gq_ref, wuq_ref, wuqc_ref, gkv_ref, wk_ref, wv_ref,
                   ckr_ref, skr_ref, cd_ref, sd_ref, lng_ref, lnb_ref, ws_ref, bs_ref,
                   qmt_ref, km_ref, vmt_ref, qdt_ref, kd_ref, vdt_ref, cm_ref):
    r = _mod_row(pl.program_id(1))
    sh1 = mod_ref[pl.ds(r, 1), 0:D]
    sc1 = mod_ref[pl.ds(r, 1), D:2 * D]
    h = (_stream_tile(x_ref, c_ref, pl.program_id(1)) * (1.0 + sc1) + sh1).astype(BF16)
    z = _dot(h, win_ref[...])

    ckr = ckr_ref[...]
    skr = skr_ref[...]
    cd = cd_ref[...]
    sd = sd_ref[...]

    qn = _rms(z[:, 0:Q_LORA], gq_ref[...]).astype(BF16)
    qa = _dot(qn, wuq_ref[...])
    qc = _dot(qn, wuqc_ref[...])
    for hd in range(HEADS):
        nope = qa[:, hd * 256:hd * 256 + NOPE]
        rot = qa[:, hd * 256 + NOPE:(hd + 1) * 256] * ckr + qc[:, hd * 128:(hd + 1) * 128] * skr
        qmt_ref[hd * 256:hd * 256 + NOPE, :] = (nope * MLA_QSCALE).T.astype(BF16)
        qmt_ref[hd * 256 + NOPE:(hd + 1) * 256, :] = (rot * MLA_QSCALE).T.astype(BF16)

    kvn = _rms(z[:, Q_LORA:Q_LORA + KV_LORA], gkv_ref[...]).astype(BF16)
    kn = _dot(kvn, wk_ref[...])
    ones = jnp.ones((ONES_ROWS, T), BF16)
    vt = _dot(kvn, wv_ref[...]).T.astype(BF16)
    for hd in range(HEADS):
        vmt_ref[hd * MLA_VROWS:hd * MLA_VROWS + DV, :] = vt[hd * DV:(hd + 1) * DV]
        vmt_ref[hd * MLA_VROWS + DV:(hd + 1) * MLA_VROWS, :] = ones
    kr = (z[:, 640:768] * ckr + z[:, 2048:2176] * skr).astype(BF16)
    for hd in range(HEADS):
        km_ref[:, hd * 256:hd * 256 + NOPE] = kn[:, hd * 128:(hd + 1) * 128].astype(BF16)
        km_ref[:, hd * 256 + NOPE:(hd + 1) * 256] = kr

    for blk in range(2):
        lo = blk * 128
        qdt_ref[lo:lo + 128, :] = ((z[:, 768 + lo:896 + lo] * cd + z[:, 2176 + lo:2304 + lo] * sd)
                                   * DIFF_QSCALE).T.astype(BF16)
        kd_ref[:, lo:lo + 128] = (z[:, 1024 + lo:1152 + lo] * cd
                                  + z[:, 2432 + lo:2560 + lo] * sd).astype(BF16)
    vt = z[:, 1280:1536].T.astype(BF16)
    for hd in range(HEADS):
        vdt_ref[hd * DIFF_VROWS:hd * DIFF_VROWS + DIFF_DV, :] = vt[hd * DIFF_DV:(hd + 1) * DIFF_DV]
        vdt_ref[hd * DIFF_VROWS + DIFF_DV:(hd + 1) * DIFF_VROWS, :] = ones

    zc = z[:, 1536:2048]
    zc = 0.5 * zc * (1.0 + jnp.tanh(math.sqrt(2.0 / math.pi) * (zc + 0.044715 * (zc * zc * zc))))
    u = zc[:, :CHUNK_W]
    v = _ln(zc[:, CHUNK_W:], lng_ref[...], lnb_ref[...]).astype(BF16)
    group = lax.broadcasted_iota(jnp.int32, (CHUNK, CHUNK_W), 1) // (CHUNK_W // GROUPS)
    ws = ws_ref[...]
    for n in range(T // CHUNK):
        mixed_all = _dot(ws, v[n * CHUNK:(n + 1) * CHUNK, :])
        mixed = mixed_all[0:CHUNK]
        for g in range(1, GROUPS):
            mixed = jnp.where(group == g, mixed_all[g * CHUNK:(g + 1) * CHUNK], mixed)
        cm_ref[n * CHUNK:(n + 1) * CHUNK, :] = (u[n * CHUNK:(n + 1) * CHUNK, :]
                                                * (mixed + bs_ref[...])).astype(BF16)


def _inproj(l, x, c, mod, w):
    nb = x.shape[0]
    row = lambda width: pl.BlockSpec((None, T, width), lambda b, i: (b, i, 0))
    tab = pl.BlockSpec((T, 128), lambda b, i: (i, 0))
    rows = lambda width: jax.ShapeDtypeStruct((nb, ST, width), BF16)
    cols = lambda width: jax.ShapeDtypeStruct((nb, NT, width, T), BF16)
    col = lambda width: pl.BlockSpec((None, None, width, T), lambda b, i: (b, i, 0, 0))
    outs = [cols(HEADS * 256), rows(1024), cols(HEADS * MLA_VROWS), cols(DIFF_W), rows(256),
            cols(HEADS * DIFF_VROWS), rows(256)]
    return pl.pallas_call(
        _inproj_kernel,
        out_shape=outs,
        grid=(nb, NT),
        in_specs=_stream_specs(x, c) + [
                  _layer_spec(l, (MOD_ROWS, 6 * D)), _layer_spec(l, (D, ZW)),
                  _layer_spec(l, (1, Q_LORA)), _layer_spec(l, (Q_LORA, HEADS * 256)), _layer_spec(l, (Q_LORA, HEADS * 128)),
                  _layer_spec(l, (1, KV_LORA)), _layer_spec(l, (KV_LORA, HEADS * NOPE)), _layer_spec(l, (KV_LORA, HEADS * DV)),
                  tab, tab, tab, tab,
                  _layer_spec(l, (1, CHUNK_W)), _layer_spec(l, (1, CHUNK_W)),
                  _layer_spec(l, (GROUPS * CHUNK, CHUNK)), _layer_spec(l, (CHUNK, CHUNK_W))],
        out_specs=[col(HEADS * 256), row(1024), col(HEADS * MLA_VROWS), col(DIFF_W), row(256),
                   col(HEADS * DIFF_VROWS), row(256)],
        compiler_params=_params(2),
        name="inproj",
    )(x, c, mod, w["win"], w["gq"], w["wuq"], w["wuqc"], w["gkv"], w["wk"], w["wv"],
      w["ckr"], w["skr"], w["cd"], w["sd"], w["lng"], w["lnb"], w["ws"], w["bs"])


def _key_rows(c):
    if isinstance(c, int):
        return pl.ds(c * T, T)
    return pl.ds(pl.multiple_of(c * T, T), T)


def _attend(n, qk, qk_next_tile, vt, s_ref, cmax_ref, m_ref, acc_ref, tile):
    m_ref[...] = jnp.full(m_ref.shape, -jnp.inf, F32)
    acc_ref[...] = jnp.zeros(acc_ref.shape, F32)

    def scores(j, c, qk=qk):
        s = qk(j, c)
        s_ref[j] = s
        cmax_ref[j] = jnp.max(s, axis=0, keepdims=True)

    def update(j, c, mask_padding=False):
        s = s_ref[j]
        if mask_padding:
            key = lax.broadcasted_iota(jnp.int32, (T, T), 0)
            s = jnp.where(key < C, s, -jnp.inf)
        m_prev = m_ref[j]
        m_new = jnp.maximum(m_prev, cmax_ref[j])
        alpha = jnp.exp2(m_prev - m_new)
        p = jnp.exp2(s - m_new).astype(BF16)
        acc_ref[j] = alpha * acc_ref[j] + _dot(vt(j, c), p)
        m_ref[j] = m_new

    @pl.when(tile == 0)
    def _():
        for j in range(n - 1):
            scores(j, 0)

    @pl.when(tile < NT_LAT)
    def _():
        def body(c, carry):
            scores(n - 1, c)
            for j in range(n - 1):
                update(j, c)
                scores(j, c + 1)
            update(n - 1, c)
            return carry

        lax.fori_loop(0, NT_LAT, body, 0, unroll=4)

    scores(n - 1, NT_LAT)
    first_chunk_of_next = jnp.where(tile + 1 == NT_LAT, NT_LAT, 0)
    for j in range(n):
        update(j, NT_LAT, mask_padding=True)
        if j < n - 1:
            scores(j, first_chunk_of_next, qk_next_tile)


def _next_tile_spec(block, n_q_tiles):
    return pl.BlockSpec(block, lambda b, i: (b, jnp.minimum(i + 1, n_q_tiles - 1), 0, 0))


def _mla_attn_kernel(qt_ref, qt_next_ref, k_ref, vt_ref, o_ref, s_ref, cmax_ref, m_ref, acc_ref):
    def scores_with(q_ref):
        return lambda j, c: _dot(k_ref[_key_rows(c), j * 256:(j + 1) * 256], q_ref[j * 256:(j + 1) * 256, :])

    _attend(HEADS, scores_with(qt_ref), scores_with(qt_next_ref),
            lambda j, c: vt_ref[c, j * MLA_VROWS:(j + 1) * MLA_VROWS, :],
            s_ref, cmax_ref, m_ref, acc_ref, pl.program_id(1))
    for hd in range(HEADS):
        o = acc_ref[hd, 0:DV, :] / acc_ref[hd, DV:DV + 1, :]
        o_ref[:, hd * DV:(hd + 1) * DV] = o.T.astype(BF16)


def _mla_attention(qm, km, vmt, n_q_tiles):
    nb = qm.shape[0]
    resident = lambda shape: pl.BlockSpec((None,) + shape, lambda b, i: (b,) + (0,) * len(shape),
                                          pipeline_mode=pl.Buffered(1))
    return pl.pallas_call(
        _mla_attn_kernel,
        out_shape=jax.ShapeDtypeStruct((nb, n_q_tiles * T, HEADS * DV), BF16),
        grid=(nb, n_q_tiles),
        in_specs=[pl.BlockSpec((None, None, HEADS * 256, T), lambda b, i: (b, i, 0, 0)),
                  _next_tile_spec((None, None, HEADS * 256, T), n_q_tiles),
                  resident((ST, HEADS * 256)), resident((NT, HEADS * MLA_VROWS, T))],
        out_specs=pl.BlockSpec((None, T, HEADS * DV), lambda b, i: (b, i, 0)),
        scratch_shapes=[pltpu.VMEM((HEADS, T, T), F32),
                        pltpu.VMEM((HEADS, 1, T), F32), pltpu.VMEM((HEADS, 1, T), F32),
                        pltpu.VMEM((HEADS, MLA_VROWS, T), F32)],
        compiler_params=_params(2),
        name="mla_attention",
    )(qm, qm, km, vmt)


DIFF_MAPS = 2 * HEADS


MAPS_PER_BLOCK = 128 // DIFF_DQK


def _diff_attn_kernel(lam_init, qt_ref, qt_next_ref, k_ref, vt_ref, lqk_ref, g_ref, o_ref,
                      qm_ref, qm_next_ref, s_ref, cmax_ref, m_ref, acc_ref):
    dim = lax.broadcasted_iota(jnp.int32, (128, T), 0)
    for src_ref, maps_ref in ((qt_ref, qm_ref), (qt_next_ref, qm_next_ref)):
        for j in range(DIFF_MAPS):
            blk = j // MAPS_PER_BLOCK
            qt = src_ref[blk * 128:(blk + 1) * 128, :]
            maps_ref[j] = jnp.where(dim // DIFF_DQK == j % MAPS_PER_BLOCK, qt, jnp.zeros_like(qt))

    def scores_with(maps_ref):
        def qk(j, c):
            blk = j // MAPS_PER_BLOCK
            return _dot(k_ref[_key_rows(c), blk * 128:(blk + 1) * 128], maps_ref[j])
        return qk

    _attend(DIFF_MAPS, scores_with(qm_ref), scores_with(qm_next_ref),
            lambda j, c: vt_ref[c, (j // 2) * DIFF_VROWS:(j // 2 + 1) * DIFF_VROWS, :],
            s_ref, cmax_ref, m_ref, acc_ref, pl.program_id(1))

    lqk = lqk_ref[...]
    lam = (jnp.exp(jnp.sum(lqk[0:1] * lqk[1:2], axis=-1, keepdims=True))
           - jnp.exp(jnp.sum(lqk[2:3] * lqk[3:4], axis=-1, keepdims=True)) + lam_init)
    attn = lambda j: acc_ref[j, 0:DIFF_DV, :] / acc_ref[j, DIFF_DV:DIFF_DV + 1, :]
    heads = []
    for hd in range(HEADS):
        d = attn(2 * hd) - lam * attn(2 * hd + 1)
        ms = jnp.mean(d * d, axis=0, keepdims=True)
        heads.append(d * lax.rsqrt(ms + EPS))
    out = jnp.concatenate(heads, axis=0).T
    o_ref[...] = ((out * g_ref[...]) * (1.0 - lam_init)).astype(BF16)


def _diff_attention(l, qd, kd, vdt, lqk, g_row, lam_init, n_q_tiles):
    nb = qd.shape[0]
    return pl.pallas_call(
        functools.partial(_diff_attn_kernel, lam_init),
        out_shape=jax.ShapeDtypeStruct((nb, n_q_tiles * T, DIFF_W), BF16),
        grid=(nb, n_q_tiles),
        in_specs=[pl.BlockSpec((None, None, DIFF_W, T), lambda b, i: (b, i, 0, 0)),
                  _next_tile_spec((None, None, DIFF_W, T), n_q_tiles),
                  pl.BlockSpec((None, ST, DIFF_W), lambda b, i: (b, 0, 0)),
                  pl.BlockSpec((None, NT, HEADS * DIFF_VROWS, T), lambda b, i: (b, 0, 0, 0)),
                  _layer_spec(l, (4, DIFF_DQK)), _layer_spec(l, (1, DIFF_W))],
        out_specs=pl.BlockSpec((None, T, DIFF_W), lambda b, i: (b, i, 0)),
        scratch_shapes=[pltpu.VMEM((DIFF_MAPS, 128, T), BF16), pltpu.VMEM((DIFF_MAPS, 128, T), BF16),
                        pltpu.VMEM((DIFF_MAPS, T, T), F32),
                        pltpu.VMEM((DIFF_MAPS, 1, T), F32), pltpu.VMEM((DIFF_MAPS, 1, T), F32),
                        pltpu.VMEM((DIFF_MAPS, DIFF_VROWS, T), F32)],
        compiler_params=_params(2),
        name="diff_attention",
    )(qd, qd, kd, vdt, lqk, g_row)


def _outproj_kernel(x_ref, c_ref, om_ref, od_ref, cm_ref, mod_ref, wo_ref, g_ref, b_ref, o_ref):
    r = _mod_row(pl.program_id(1))
    g1 = mod_ref[pl.ds(r, 1), 2 * D:3 * D]
    y = (_dot(om_ref[...], wo_ref[0:512, :]) + _dot(od_ref[...], wo_ref[512:768, :])
         + _dot(cm_ref[...], wo_ref[768:1024, :]))
    x = _stream_tile(x_ref, c_ref, pl.program_id(1))
    o_ref[...] = _ln(DN_ALPHA * x + g1 * y, g_ref[...], b_ref[...])


def _outproj(l, x, c, om, od, cm, mod, w, n_tiles):
    nb = x.shape[0]
    row = lambda width: pl.BlockSpec((None, T, width), lambda b, i: (b, i, 0))
    return pl.pallas_call(
        _outproj_kernel,
        out_shape=jax.ShapeDtypeStruct((nb, n_tiles * T, D), F32),
        grid=(nb, n_tiles),
        in_specs=_stream_specs(x, c) + [
                  row(512), row(256), row(256), _layer_spec(l, (MOD_ROWS, 6 * D)),
                  _layer_spec(l, (D, D)), _layer_spec(l, (1, D)), _layer_spec(l, (1, D))],
        out_specs=row(D),
        compiler_params=_params(2),
        name="outproj",
    )(x, c, om, od, cm, mod, w["wo"], w["ln1g"], w["ln1b"])


def _ffn_kernel(x_ref, xp_ref, xn_ref, mod_ref, wup_ref, cw_ref, cb_ref, wdn_ref, g_ref, b_ref,
                o_ref, h_ref, act_ref):
    i = pl.program_id(1)
    r = _mod_row(i)
    sh2 = mod_ref[pl.ds(r, 1), 3 * D:4 * D]
    sc2 = mod_ref[pl.ds(r, 1), 4 * D:5 * D]
    g2 = mod_ref[pl.ds(r, 1), 5 * D:6 * D]
    x = x_ref[...]
    modulate = lambda t: t * (1.0 + sc2) + sh2
    seq_start = jnp.logical_or(i == 0, i == NT_LAT)
    seq_end = i >= NT_LAT - 1
    h_ref[0:HALO, :] = jnp.where(seq_start, 0.0, modulate(xp_ref[...])).astype(BF16)
    hm = modulate(x)
    row = lax.broadcasted_iota(jnp.int32, (T, 1), 0)
    hm = jnp.where(jnp.logical_and(i == NT_LAT, row >= C), 0.0, hm)
    h_ref[HALO:HALO + T, :] = hm.astype(BF16)
    h_ref[HALO + T:, :] = jnp.where(seq_end, 0.0, modulate(xn_ref[...])).astype(BF16)
    h = h_ref[...]

    def conv(col):
        up = _dot(h, wup_ref[:, col:col + FF_CHUNK])
        cw = cw_ref[:, col:col + FF_CHUNK]
        prev = pltpu.roll(up, 1, 0)[HALO:HALO + T]
        nxt = pltpu.roll(up, T + 2 * HALO - 1, 0)[HALO:HALO + T]
        return prev * cw[0:1] + up[HALO:HALO + T] * cw[1:2] + nxt * cw[2:3] + cb_ref[:, col:col + FF_CHUNK]

    for j in range(D_FF // FF_CHUNK):
        gate = conv(j * FF_CHUNK)
        val = conv(D_FF + j * FF_CHUNK)
        act_ref[:, j * FF_CHUNK:(j + 1) * FF_CHUNK] = ((gate * (1.0 / (1.0 + jnp.exp(-gate)))) * val).astype(BF16)
    y = _dot(act_ref[...], wdn_ref[...])
    o_ref[...] = _ln(DN_ALPHA * x + g2 * y, g_ref[...], b_ref[...])


def _ffn(l, x1, mod, w, n_tiles):
    nb, rows, _ = x1.shape
    nh = rows // HALO
    per = T // HALO
    return pl.pallas_call(
        _ffn_kernel,
        out_shape=jax.ShapeDtypeStruct((nb, n_tiles * T, D), F32),
        grid=(nb, n_tiles),
        in_specs=[pl.BlockSpec((None, T, D), lambda b, i: (b, i, 0)),
                  pl.BlockSpec((None, HALO, D), lambda b, i: (b, jnp.maximum(i * per - 1, 0), 0)),
                  pl.BlockSpec((None, HALO, D), lambda b, i: (b, jnp.minimum((i + 1) * per, nh - 1), 0)),
                  _layer_spec(l, (MOD_ROWS, 6 * D)), _layer_spec(l, (D, 2 * D_FF)),
                  _layer_spec(l, (3, 2 * D_FF)), _layer_spec(l, (1, 2 * D_FF)), _layer_spec(l, (D_FF, D)),
                  _layer_spec(l, (1, D)), _layer_spec(l, (1, D))],
        out_specs=pl.BlockSpec((None, T, D), lambda b, i: (b, i, 0)),
        scratch_shapes=[pltpu.VMEM((T + 2 * HALO, D), BF16), pltpu.VMEM((T, D_FF), BF16)],
        compiler_params=_params(2),
        name="conv_ffn",
    )(x1, x1, x1, mod, w["wup"], w["convw"], w["convb"], w["wdn"], w["ln2g"], w["ln2b"])


def _swap_halves(n_blocks, half):
    idx = jnp.arange(n_blocks * 2 * half)
    return jnp.where((idx // half) % 2 == 0, idx + half, idx - half)


def _rope_tables():
    grid_h = S // GRID_W

    def tables(n, width):
        inv = ROPE_BASE ** (-np.arange(n, dtype=np.float64) / n)
        ar = np.arange(grid_h, dtype=np.float64)[:, None] * inv[None, :]
        ac = np.arange(GRID_W, dtype=np.float64)[:, None] * inv[None, :]
        per_row = lambda a, b: jnp.repeat(jnp.asarray(np.concatenate([a, b], -1), F32), GRID_W, axis=0)
        per_col = lambda a, b: jnp.tile(jnp.asarray(np.concatenate([a, b], -1), F32), (grid_h, 1))
        cos = jnp.concatenate([per_row(np.cos(ar), np.cos(ar)), per_col(np.cos(ac), np.cos(ac))], -1)
        sin = jnp.concatenate([per_row(-np.sin(ar), np.sin(ar)), per_col(-np.sin(ac), np.sin(ac))], -1)
        reps = width // (4 * n)
        cos, sin = jnp.tile(cos, (1, reps)), jnp.tile(sin, (1, reps))
        cos = jnp.concatenate([cos, jnp.ones((ST - S, cos.shape[1]), F32)], 0)
        sin = jnp.concatenate([sin, jnp.zeros((ST - S, sin.shape[1]), F32)], 0)
        return cos, sin

    ckr, skr = tables(ROPE // 4, ROPE)
    pad = jnp.zeros((ST, 128 - ROPE), F32)
    ckr, skr = jnp.concatenate([ckr, pad], -1), jnp.concatenate([skr, pad], -1)
    cd, sd = tables(DIFF_DQK // 4, 128)
    return ckr, skr, cd, sd


def _prepare_weights(tabs, w_in, mla_gq, mla_wuq, mla_gkv, mla_wukv, diff_lq1, diff_lk1, diff_lq2, diff_lk2,
                     diff_subln_g, sgu_ln_g, sgu_ln_b, sgu_ws, sgu_bs, w_o, ln1_g, ln1_b, ffn_wup, ffn_convw,
                     ffn_convb, ffn_wdown, ln2_g, ln2_b):
    k_r = w_in[:, :, 640:704]
    dq = w_in[:, :, 704:960]
    dk = w_in[:, :, 960:1216]
    zpad = jnp.zeros((DEPTH, D, 128 - ROPE), F32)
    p64 = _swap_halves(2, ROPE // 4)
    p256 = _swap_halves(16, DIFF_DQK // 4)
    win_wide = jnp.concatenate(
        [w_in[:, :, 0:640], k_r, zpad, dq, dk, w_in[:, :, 1216:1984],
         k_r[:, :, p64], zpad, dq[:, :, p256], dk[:, :, p256]], axis=2).astype(BF16)

    wuq = mla_wuq.reshape(DEPTH, Q_LORA, HEADS, NOPE + ROPE)
    zq = jnp.zeros((DEPTH, Q_LORA, HEADS, 128 - ROPE), F32)
    wuq_main = jnp.concatenate([wuq, zq], -1).reshape(DEPTH, Q_LORA, HEADS * 256).astype(BF16)
    wuq_part = jnp.concatenate([wuq[..., NOPE:][..., p64], zq], -1).reshape(DEPTH, Q_LORA, HEADS * 128).astype(BF16)
    wukv = mla_wukv.reshape(DEPTH, KV_LORA, HEADS, NOPE + DV)
    ckr, skr, cd, sd = tabs
    row = lambda v: v[:, None, :]
    bias = jnp.repeat(jnp.swapaxes(sgu_bs, 1, 2), CHUNK_W // GROUPS, axis=2)
    return dict(
        win=win_wide, gq=row(mla_gq), wuq=wuq_main, wuqc=wuq_part, gkv=row(mla_gkv),
        wk=wukv[..., :NOPE].reshape(DEPTH, KV_LORA, HEADS * NOPE).astype(BF16),
        wv=wukv[..., NOPE:].reshape(DEPTH, KV_LORA, HEADS * DV).astype(BF16),
        ckr=ckr, skr=skr, cd=cd, sd=sd, lng=row(sgu_ln_g), lnb=row(sgu_ln_b),
        ws=sgu_ws.reshape(DEPTH, GROUPS * CHUNK, CHUNK).astype(BF16), bs=bias,
        lqk=jnp.stack([diff_lq1, diff_lk1, diff_lq2, diff_lk2], axis=1),
        subg=row(jnp.tile(diff_subln_g, (1, HEADS))),
        wo=w_o.astype(BF16), ln1g=row(ln1_g), ln1b=row(ln1_b),
        wup=ffn_wup.astype(BF16), convw=ffn_convw, convb=row(ffn_convb),
        wdn=ffn_wdown.astype(BF16), ln2g=row(ln2_g), ln2b=row(ln2_b))


def kernel(x, c, ctx, c_ctx, ada_w, ada_b, w_in, mla_gq, mla_wuq, mla_gkv, mla_wukv, diff_lq1, diff_lk1, diff_lq2, diff_lk2, diff_subln_g, sgu_ln_g, sgu_ln_b, sgu_ws, sgu_bs, w_o, ln1_g, ln1_b, ffn_wup, ffn_convw, ffn_convb, ffn_wdown, ln2_g, ln2_b):
    nb = x.shape[0]
    cvec = jnp.concatenate([c, c_ctx[None], jnp.zeros((MOD_ROWS - nb - 1, D), F32)], 0)
    mod = _modulation(cvec, ada_w, ada_b)
    tabs = _rope_tables()
    w = _prepare_weights(tabs, w_in, mla_gq, mla_wuq, mla_gkv, mla_wukv, diff_lq1, diff_lk1, diff_lq2, diff_lk2,
                         diff_subln_g, sgu_ln_g, sgu_ln_b, sgu_ws, sgu_bs, w_o, ln1_g, ln1_b, ffn_wup, ffn_convw,
                         ffn_convb, ffn_wdown, ln2_g, ln2_b)
    xs, xc = x, ctx
    for l in range(DEPTH):
        n_tiles = NT if l < DEPTH - 1 else NT_LAT
        lam_init = 0.8 - 0.6 * math.exp(-0.3 * l)
        qm, km, vmt, qd, kd, vdt, cm = _inproj(l, xs, xc, mod, w)
        om = _mla_attention(qm, km, vmt, n_tiles)
        od = _diff_attention(l, qd, kd, vdt, w["lqk"], w["subg"], lam_init, n_tiles)
        x1 = _outproj(l, xs, xc, om, od, cm, mod, w, n_tiles)
        xs = xc = _ffn(l, x1, mod, w, n_tiles)
    return xs
```

```python
import functools
import math

import jax
import jax.numpy as jnp
import numpy as np
from jax import lax
from jax.experimental import pallas as pl
from jax.experimental.pallas import tpu as pltpu

F32 = jnp.float32
BF16 = jnp.bfloat16

D = 1024
S = 8192
C = 256
GRID_W = 64
DEPTH = 2
T = 512
ST = S + 2 * C
NT = ST // T
NT_LAT = S // T
HEADS = 4
Q_LORA = 384
KV_LORA = 256
NOPE = 128
ROPE = 64
DV = 128
DIFF_W = 256
DIFF_DQK = 32
DIFF_DV = 64
CHUNK_W = 256
CHUNK = 128
GROUPS = 4
D_FF = 2816
FF_CHUNK = 256
HALO = 16
LN_PARTS = 2
ONES_ROWS = 16
MLA_VROWS = DV + ONES_ROWS
DIFF_VROWS = DIFF_DV + ONES_ROWS
EPS = 1e-6
ROPE_BASE = 10000.0
DN_ALPHA = (2 * DEPTH) ** 0.25
MLA_SCALE = (NOPE + ROPE) ** -0.5
DIFF_SCALE = DIFF_DQK ** -0.5
LOG2E = math.log2(math.e)
MLA_QSCALE = MLA_SCALE * LOG2E
DIFF_QSCALE = DIFF_SCALE * LOG2E
LANES = 128
QK = NOPE + LANES
MLA_W = HEADS * DV
Z_KV = Q_LORA
Z_KR = Z_KV + KV_LORA
Z_DQ = Z_KR + LANES
Z_DK = Z_DQ + DIFF_W
Z_DV = Z_DK + DIFF_W
Z_CH = Z_DV + DIFF_W
ZW = Z_CH + 2 * CHUNK_W
MOD_ROWS = 8
CTX_ROW = 2
VMEM_LIMIT = 56 * 1024 * 1024

_NT_DIMS = (((1,), (1,)), ((), ()))


def _dot(a, b):
    return jnp.dot(a, b, preferred_element_type=F32)


def _rms(x, g):
    return (x * lax.rsqrt(jnp.mean(x * x, axis=-1, keepdims=True) + EPS)) * g


def _ln(x, g, b):
    mu = jnp.mean(x, axis=-1, keepdims=True)
    xc = x - mu
    var = jnp.mean(xc * xc, axis=-1, keepdims=True)
    return (xc * lax.rsqrt(var + EPS)) * g + b


def _params(n_axes):
    return pltpu.CompilerParams(dimension_semantics=("arbitrary",) * n_axes,
                                vmem_limit_bytes=VMEM_LIMIT)


def _layer_spec(l, shape):
    nd = len(shape)
    return pl.BlockSpec((None,) + shape, lambda *_: (l,) + (0,) * nd, pipeline_mode=pl.Buffered(1))


MOD_BLOCK = 1536


def _mod_kernel(c_ref, w_ref, b_ref, o_ref):
    c = c_ref[...]
    sc = (c * (1.0 / (1.0 + jnp.exp(-c)))).astype(BF16)
    o_ref[...] = _dot(sc, w_ref[...].astype(BF16)) + b_ref[...]


def _modulation(cvec, ada_w, ada_b):
    return pl.pallas_call(
        _mod_kernel,
        out_shape=jax.ShapeDtypeStruct((DEPTH, MOD_ROWS, 6 * D), F32),
        grid=(DEPTH, 6 * D // MOD_BLOCK),
        in_specs=[pl.BlockSpec((MOD_ROWS, D), lambda l, j: (0, 0)),
                  pl.BlockSpec((None, D, MOD_BLOCK), lambda l, j: (l, 0, j)),
                  pl.BlockSpec((None, 1, MOD_BLOCK), lambda l, j: (l, 0, j))],
        out_specs=pl.BlockSpec((None, MOD_ROWS, MOD_BLOCK), lambda l, j: (l, 0, j)),
        compiler_params=_params(2),
        name="modulation",
    )(cvec, ada_w, ada_b.reshape(DEPTH, 1, 6 * D))


def _stream_tile(x_ref, c_ref, tile):
    return jnp.concatenate([_stream_half(x_ref, c_ref, tile, half) for half in range(T // C)], axis=0)


def _stream_half(x_ref, c_ref, tile, half):
    context = c_ref[...] if half == 0 else jnp.broadcast_to(c_ref[C - 1:C, :], (C, D))
    return jnp.where(tile == NT_LAT, context, x_ref[half * C:(half + 1) * C, :])


def _stream_specs(x, c):
    ctx_block = 0 if c.shape[1] == C else S // C
    return [pl.BlockSpec((None, T, D), lambda b, i: (b, jnp.minimum(i, NT_LAT - 1), 0)),
            pl.BlockSpec((None, C, D), lambda b, i: (b, ctx_block, 0))]


def _mod_row(tile):
    return jnp.where(tile == NT_LAT, CTX_ROW, pl.program_id(0))


def _swap_lane_halves(x, half):
    lane = lax.broadcasted_iota(jnp.int32, x.shape, 1)
    return jnp.where(lane % (2 * half) < half, pltpu.roll(x, LANES - half, 1), pltpu.roll(x, half, 1))


def _rotate(x, cos, sin_signed, half):
    return x * cos + _swap_lane_halves(x, half) * sin_signed


def _inproj_kernel(x_ref, c_ref, mod_ref, win_ref, gq_ref, wuq_ref, gkv_ref, wk_ref, wv_ref,
                   ckr_ref, skr_ref, cd_ref, sd_ref, lng_ref, lnb_ref, ws_ref, bs_ref,
                   qmt_ref, km_ref, vmt_ref, qdt_ref, kd_ref, vdt_ref, cm_ref):
    r = _mod_row(pl.program_id(1))
    sh1 = mod_ref[pl.ds(r, 1), 0:D]
    sc1 = mod_ref[pl.ds(r, 1), D:2 * D]
    group = lax.broadcasted_iota(jnp.int32, (CHUNK, CHUNK_W), 1) // (CHUNK_W // GROUPS)
    ones = jnp.ones((ONES_ROWS, T), BF16)
    h = (_stream_tile(x_ref, c_ref, pl.program_id(1)) * (1.0 + sc1) + sh1).astype(BF16)
    z = _dot(h, win_ref[...])
    ckr = ckr_ref[...]
    skr = skr_ref[...]
    cd = cd_ref[...]
    sd = sd_ref[...]

    qn = _rms(z[:, 0:Q_LORA], gq_ref[...]).astype(BF16)
    qa = _dot(qn, wuq_ref[...])
    for hd in range(HEADS):
        nope = qa[:, hd * QK:hd * QK + NOPE]
        rot = _rotate(qa[:, hd * QK + NOPE:(hd + 1) * QK], ckr, skr, ROPE // 4)
        qmt_ref[hd * QK:hd * QK + NOPE, :] = (nope * MLA_QSCALE).T.astype(BF16)
        qmt_ref[hd * QK + NOPE:(hd + 1) * QK, :] = (rot * MLA_QSCALE).T.astype(BF16)

    kvn = _rms(z[:, Q_LORA:Q_LORA + KV_LORA], gkv_ref[...]).astype(BF16)
    kn = _dot(kvn, wk_ref[...])
    vt = _dot(kvn, wv_ref[...]).T.astype(BF16)
    for hd in range(HEADS):
        vmt_ref[hd * MLA_VROWS:hd * MLA_VROWS + DV, :] = vt[hd * DV:(hd + 1) * DV]
        vmt_ref[hd * MLA_VROWS + DV:(hd + 1) * MLA_VROWS, :] = ones
    kr = _rotate(z[:, Z_KR:Z_DQ], ckr, skr, ROPE // 4).astype(BF16)
    for hd in range(HEADS):
        km_ref[:, hd * QK:hd * QK + NOPE] = kn[:, hd * NOPE:(hd + 1) * NOPE].astype(BF16)
        km_ref[:, hd * QK + NOPE:(hd + 1) * QK] = kr

    for lo in range(0, DIFF_W, LANES):
        qdt_ref[lo:lo + LANES, :] = (_rotate(z[:, Z_DQ + lo:Z_DQ + lo + LANES], cd, sd, DIFF_DQK // 4)
                                     * DIFF_QSCALE).T.astype(BF16)
        kd_ref[:, lo:lo + LANES] = _rotate(z[:, Z_DK + lo:Z_DK + lo + LANES], cd, sd, DIFF_DQK // 4).astype(BF16)
    vt = z[:, Z_DV:Z_CH].T.astype(BF16)
    for hd in range(HEADS):
        vdt_ref[hd * DIFF_VROWS:hd * DIFF_VROWS + DIFF_DV, :] = vt[hd * DIFF_DV:(hd + 1) * DIFF_DV]
        vdt_ref[hd * DIFF_VROWS + DIFF_DV:(hd + 1) * DIFF_VROWS, :] = ones

    zc = z[:, Z_CH:ZW]
    zc = 0.5 * zc * (1.0 + jnp.tanh(math.sqrt(2.0 / math.pi) * (zc + 0.044715 * (zc * zc * zc))))
    u = zc[:, :CHUNK_W]
    v = _ln(zc[:, CHUNK_W:], lng_ref[...], lnb_ref[...]).astype(BF16)
    for n in range(T // CHUNK):
        mixed_all = _dot(ws_ref[...], v[n * CHUNK:(n + 1) * CHUNK, :])
        mixed = mixed_all[0:CHUNK]
        for g in range(1, GROUPS):
            mixed = jnp.where(group == g, mixed_all[g * CHUNK:(g + 1) * CHUNK], mixed)
        cm_ref[n * CHUNK:(n + 1) * CHUNK, :] = (u[n * CHUNK:(n + 1) * CHUNK, :]
                                                * (mixed + bs_ref[...])).astype(BF16)


def _inproj(l, x, c, mod, w):
    nb = x.shape[0]
    row = lambda width: pl.BlockSpec((None, T, width), lambda b, i: (b, i, 0))
    tab = pl.BlockSpec((T, 128), lambda b, i: (i, 0))
    rows = lambda width: jax.ShapeDtypeStruct((nb, ST, width), BF16)
    cols = lambda width: jax.ShapeDtypeStruct((nb, NT, width, T), BF16)
    col = lambda width: pl.BlockSpec((None, None, width, T), lambda b, i: (b, i, 0, 0))
    outs = [cols(HEADS * QK), rows(HEADS * QK), cols(HEADS * MLA_VROWS), cols(DIFF_W), rows(DIFF_W),
            cols(HEADS * DIFF_VROWS), rows(CHUNK_W)]
    return pl.pallas_call(
        _inproj_kernel,
        out_shape=outs,
        grid=(nb, NT),
        in_specs=_stream_specs(x, c) + [
                  _layer_spec(l, (MOD_ROWS, 6 * D)), _layer_spec(l, (D, ZW)),
                  _layer_spec(l, (1, Q_LORA)), _layer_spec(l, (Q_LORA, HEADS * QK)),
                  _layer_spec(l, (1, KV_LORA)), _layer_spec(l, (KV_LORA, HEADS * NOPE)), _layer_spec(l, (KV_LORA, HEADS * DV)),
                  tab, tab, tab, tab,
                  _layer_spec(l, (1, CHUNK_W)), _layer_spec(l, (1, CHUNK_W)),
                  _layer_spec(l, (GROUPS * CHUNK, CHUNK)), _layer_spec(l, (CHUNK, CHUNK_W))],
        out_specs=[col(HEADS * QK), row(HEADS * QK), col(HEADS * MLA_VROWS), col(DIFF_W), row(DIFF_W),
                   col(HEADS * DIFF_VROWS), row(CHUNK_W)],
        compiler_params=_params(2),
        name="inproj",
    )(x, c, mod, w["win"], w["gq"], w["wuq"], w["gkv"], w["wk"], w["wv"],
      w["ckr"], w["skr"], w["cd"], w["sd"], w["lng"], w["lnb"], w["ws"], w["bs"])


def _key_rows(c):
    if isinstance(c, int):
        return pl.ds(c * T, T)
    return pl.ds(pl.multiple_of(c * T, T), T)


def _attend(n, qk, qk_next_tile, vt, s_ref, cmax_ref, m_ref, acc_ref, tile):
    m_ref[...] = jnp.full(m_ref.shape, -jnp.inf, F32)
    acc_ref[...] = jnp.zeros(acc_ref.shape, F32)

    def scores(j, c, qk=qk):
        s = qk(j, c)
        s_ref[j] = s
        cmax_ref[j] = jnp.max(s, axis=0, keepdims=True)

    def update(j, c, mask_padding=False):
        s = s_ref[j]
        if mask_padding:
            key = lax.broadcasted_iota(jnp.int32, (T, T), 0)
            s = jnp.where(key < C, s, -jnp.inf)
        m_prev = m_ref[j]
        m_new = jnp.maximum(m_prev, cmax_ref[j])
        alpha = jnp.exp2(m_prev - m_new)
        p = jnp.exp2(s - m_new).astype(BF16)
        acc_ref[j] = alpha * acc_ref[j] + _dot(vt(j, c), p)
        m_ref[j] = m_new

    @pl.when(tile == 0)
    def _():
        for j in range(n - 1):
            scores(j, 0)

    @pl.when(tile < NT_LAT)
    def _():
        def body(c, carry):
            scores(n - 1, c)
            for j in range(n - 1):
                update(j, c)
                scores(j, c + 1)
            update(n - 1, c)
            return carry

        lax.fori_loop(0, NT_LAT, body, 0, unroll=4)

    scores(n - 1, NT_LAT)
    first_chunk_of_next = jnp.where(tile + 1 == NT_LAT, NT_LAT, 0)
    for j in range(n):
        update(j, NT_LAT, mask_padding=True)
        if j < n - 1:
            scores(j, first_chunk_of_next, qk_next_tile)


def _next_tile_spec(block, n_q_tiles):
    return pl.BlockSpec(block, lambda b, i: (b, jnp.minimum(i + 1, n_q_tiles - 1), 0, 0))


def _mla_attn_kernel(qt_ref, qt_next_ref, k_ref, vt_ref, o_ref, s_ref, cmax_ref, m_ref, acc_ref):
    def scores_with(q_ref):
        return lambda j, c: _dot(k_ref[_key_rows(c), j * QK:(j + 1) * QK], q_ref[j * QK:(j + 1) * QK, :])

    _attend(HEADS, scores_with(qt_ref), scores_with(qt_next_ref),
            lambda j, c: vt_ref[c, j * MLA_VROWS:(j + 1) * MLA_VROWS, :],
            s_ref, cmax_ref, m_ref, acc_ref, pl.program_id(1))
    for hd in range(HEADS):
        o = acc_ref[hd, 0:DV, :] / acc_ref[hd, DV:DV + 1, :]
        o_ref[:, hd * DV:(hd + 1) * DV] = o.T.astype(BF16)


def _mla_attention(qm, km, vmt, n_q_tiles):
    nb = qm.shape[0]
    resident = lambda shape: pl.BlockSpec((None,) + shape, lambda b, i: (b,) + (0,) * len(shape),
                                          pipeline_mode=pl.Buffered(1))
    return pl.pallas_call(
        _mla_attn_kernel,
        out_shape=jax.ShapeDtypeStruct((nb, n_q_tiles * T, HEADS * DV), BF16),
        grid=(nb, n_q_tiles),
        in_specs=[pl.BlockSpec((None, None, HEADS * QK, T), lambda b, i: (b, i, 0, 0)),
                  _next_tile_spec((None, None, HEADS * QK, T), n_q_tiles),
                  resident((ST, HEADS * QK)), resident((NT, HEADS * MLA_VROWS, T))],
        out_specs=pl.BlockSpec((None, T, HEADS * DV), lambda b, i: (b, i, 0)),
        scratch_shapes=[pltpu.VMEM((HEADS, T, T), F32),
                        pltpu.VMEM((HEADS, 1, T), F32), pltpu.VMEM((HEADS, 1, T), F32),
                        pltpu.VMEM((HEADS, MLA_VROWS, T), F32)],
        compiler_params=_params(2),
        name="mla_attention",
    )(qm, qm, km, vmt)


DIFF_MAPS = 2 * HEADS


MAPS_PER_BLOCK = LANES // DIFF_DQK


def _diff_attn_kernel(lam_init, qt_ref, qt_next_ref, k_ref, vt_ref, lqk_ref, g_ref, o_ref,
                      qm_ref, qm_next_ref, s_ref, cmax_ref, m_ref, acc_ref):
    dim = lax.broadcasted_iota(jnp.int32, (LANES, T), 0)
    for src_ref, maps_ref in ((qt_ref, qm_ref), (qt_next_ref, qm_next_ref)):
        for j in range(DIFF_MAPS):
            blk = j // MAPS_PER_BLOCK
            qt = src_ref[blk * LANES:(blk + 1) * LANES, :]
            maps_ref[j] = jnp.where(dim // DIFF_DQK == j % MAPS_PER_BLOCK, qt, jnp.zeros_like(qt))

    def scores_with(maps_ref):
        def qk(j, c):
            blk = j // MAPS_PER_BLOCK
            return _dot(k_ref[_key_rows(c), blk * LANES:(blk + 1) * LANES], maps_ref[j])
        return qk

    _attend(DIFF_MAPS, scores_with(qm_ref), scores_with(qm_next_ref),
            lambda j, c: vt_ref[c, (j // 2) * DIFF_VROWS:(j // 2 + 1) * DIFF_VROWS, :],
            s_ref, cmax_ref, m_ref, acc_ref, pl.program_id(1))

    lqk = lqk_ref[...]
    lam = (jnp.exp(jnp.sum(lqk[0:1] * lqk[1:2], axis=-1, keepdims=True))
           - jnp.exp(jnp.sum(lqk[2:3] * lqk[3:4], axis=-1, keepdims=True)) + lam_init)
    attn = lambda j: acc_ref[j, 0:DIFF_DV, :] / acc_ref[j, DIFF_DV:DIFF_DV + 1, :]
    heads = []
    for hd in range(HEADS):
        d = attn(2 * hd) - lam * attn(2 * hd + 1)
        ms = jnp.mean(d * d, axis=0, keepdims=True)
        heads.append(d * lax.rsqrt(ms + EPS))
    out = jnp.concatenate(heads, axis=0).T
    o_ref[...] = ((out * g_ref[...]) * (1.0 - lam_init)).astype(BF16)


def _diff_attention(l, qd, kd, vdt, lqk, g_row, lam_init, n_q_tiles):
    nb = qd.shape[0]
    return pl.pallas_call(
        functools.partial(_diff_attn_kernel, lam_init),
        out_shape=jax.ShapeDtypeStruct((nb, n_q_tiles * T, DIFF_W), BF16),
        grid=(nb, n_q_tiles),
        in_specs=[pl.BlockSpec((None, None, DIFF_W, T), lambda b, i: (b, i, 0, 0)),
                  _next_tile_spec((None, None, DIFF_W, T), n_q_tiles),
                  pl.BlockSpec((None, ST, DIFF_W), lambda b, i: (b, 0, 0)),
                  pl.BlockSpec((None, NT, HEADS * DIFF_VROWS, T), lambda b, i: (b, 0, 0, 0)),
                  _layer_spec(l, (4, DIFF_DQK)), _layer_spec(l, (1, DIFF_W))],
        out_specs=pl.BlockSpec((None, T, DIFF_W), lambda b, i: (b, i, 0)),
        scratch_shapes=[pltpu.VMEM((DIFF_MAPS, LANES, T), BF16), pltpu.VMEM((DIFF_MAPS, LANES, T), BF16),
                        pltpu.VMEM((DIFF_MAPS, T, T), F32),
                        pltpu.VMEM((DIFF_MAPS, 1, T), F32), pltpu.VMEM((DIFF_MAPS, 1, T), F32),
                        pltpu.VMEM((DIFF_MAPS, DIFF_VROWS, T), F32)],
        compiler_params=_params(2),
        name="diff_attention",
    )(qd, qd, kd, vdt, lqk, g_row)


def _outproj_kernel(x_ref, c_ref, om_ref, od_ref, cm_ref, mod_ref, wo_ref, g_ref, b_ref, o_ref):
    r = _mod_row(pl.program_id(1))
    g1 = mod_ref[pl.ds(r, 1), 2 * D:3 * D]
    for half in range(T // C):
        rows = pl.ds(half * C, C)
        y = (_dot(om_ref[rows, :], wo_ref[0:MLA_W, :]) + _dot(od_ref[rows, :], wo_ref[MLA_W:MLA_W + DIFF_W, :])
             + _dot(cm_ref[rows, :], wo_ref[MLA_W + DIFF_W:D, :]))
        x = _stream_half(x_ref, c_ref, pl.program_id(1), half)
        o_ref[rows, :] = _ln(DN_ALPHA * x + g1 * y, g_ref[...], b_ref[...])


def _outproj(l, x, c, om, od, cm, mod, w, n_tiles):
    nb = x.shape[0]
    row = lambda width: pl.BlockSpec((None, T, width), lambda b, i: (b, i, 0))
    return pl.pallas_call(
        _outproj_kernel,
        out_shape=jax.ShapeDtypeStruct((nb, n_tiles * T, D), F32),
        grid=(nb, n_tiles),
        in_specs=_stream_specs(x, c) + [
                  row(MLA_W), row(DIFF_W), row(CHUNK_W), _layer_spec(l, (MOD_ROWS, 6 * D)),
                  _layer_spec(l, (D, D)), _layer_spec(l, (1, D)), _layer_spec(l, (1, D))],
        out_specs=row(D),
        compiler_params=_params(2),
        name="outproj",
    )(x, c, om, od, cm, mod, w["wo"], w["ln1g"], w["ln1b"])


def _ffn_kernel(x_ref, xp_ref, xn_ref, mod_ref, wup_ref, cw_ref, cb_ref, wdn_ref, g_ref, b_ref,
                o_ref, h_ref, act_ref):
    i = pl.program_id(1)
    r = _mod_row(i)
    sh2 = mod_ref[pl.ds(r, 1), 3 * D:4 * D]
    sc2 = mod_ref[pl.ds(r, 1), 4 * D:5 * D]
    g2 = mod_ref[pl.ds(r, 1), 5 * D:6 * D]
    x = x_ref[...]
    modulate = lambda t: t * (1.0 + sc2) + sh2
    seq_start = jnp.logical_or(i == 0, i == NT_LAT)
    seq_end = i >= NT_LAT - 1
    h_ref[0:HALO, :] = jnp.where(seq_start, 0.0, modulate(xp_ref[...])).astype(BF16)
    hm = modulate(x)
    row = lax.broadcasted_iota(jnp.int32, (T, 1), 0)
    hm = jnp.where(jnp.logical_and(i == NT_LAT, row >= C), 0.0, hm)
    h_ref[HALO:HALO + T, :] = hm.astype(BF16)
    h_ref[HALO + T:, :] = jnp.where(seq_end, 0.0, modulate(xn_ref[...])).astype(BF16)
    h = h_ref[...]

    def conv(col):
        up = _dot(h, wup_ref[:, col:col + FF_CHUNK])
        cw = cw_ref[:, col:col + FF_CHUNK]
        prev = pltpu.roll(up, 1, 0)[HALO:HALO + T]
        nxt = pltpu.roll(up, T + 2 * HALO - 1, 0)[HALO:HALO + T]
        return prev * cw[0:1] + up[HALO:HALO + T] * cw[1:2] + nxt * cw[2:3] + cb_ref[:, col:col + FF_CHUNK]

    for j in range(D_FF // FF_CHUNK):
        gate = conv(j * FF_CHUNK)
        val = conv(D_FF + j * FF_CHUNK)
        act_ref[:, j * FF_CHUNK:(j + 1) * FF_CHUNK] = ((gate * (1.0 / (1.0 + jnp.exp(-gate)))) * val).astype(BF16)
    for part in range(LN_PARTS):
        rows = pl.ds(part * (T // LN_PARTS), T // LN_PARTS)
        y = _dot(act_ref[rows, :], wdn_ref[...])
        o_ref[rows, :] = _ln(DN_ALPHA * x_ref[rows, :] + g2 * y, g_ref[...], b_ref[...])


def _ffn(l, x1, mod, w, n_tiles):
    nb, rows, _ = x1.shape
    nh = rows // HALO
    per = T // HALO
    return pl.pallas_call(
        _ffn_kernel,
        out_shape=jax.ShapeDtypeStruct((nb, n_tiles * T, D), F32),
        grid=(nb, n_tiles),
        in_specs=[pl.BlockSpec((None, T, D), lambda b, i: (b, i, 0)),
                  pl.BlockSpec((None, HALO, D), lambda b, i: (b, jnp.maximum(i * per - 1, 0), 0)),
                  pl.BlockSpec((None, HALO, D), lambda b, i: (b, jnp.minimum((i + 1) * per, nh - 1), 0)),
                  _layer_spec(l, (MOD_ROWS, 6 * D)), _layer_spec(l, (D, 2 * D_FF)),
                  _layer_spec(l, (3, 2 * D_FF)), _layer_spec(l, (1, 2 * D_FF)), _layer_spec(l, (D_FF, D)),
                  _layer_spec(l, (1, D)), _layer_spec(l, (1, D))],
        out_specs=pl.BlockSpec((None, T, D), lambda b, i: (b, i, 0)),
        scratch_shapes=[pltpu.VMEM((T + 2 * HALO, D), BF16), pltpu.VMEM((T, D_FF), BF16)],
        compiler_params=_params(2),
        name="conv_ffn",
    )(x1, x1, x1, mod, w["wup"], w["convw"], w["convb"], w["wdn"], w["ln2g"], w["ln2b"])


def _rope_tables():
    grid_h = S // GRID_W

    def tables(n, width):
        inv = ROPE_BASE ** (-np.arange(n, dtype=np.float64) / n)
        ar = np.arange(grid_h, dtype=np.float64)[:, None] * inv[None, :]
        ac = np.arange(GRID_W, dtype=np.float64)[:, None] * inv[None, :]
        per_row = lambda a, b: jnp.repeat(jnp.asarray(np.concatenate([a, b], -1), F32), GRID_W, axis=0)
        per_col = lambda a, b: jnp.tile(jnp.asarray(np.concatenate([a, b], -1), F32), (grid_h, 1))
        cos = jnp.concatenate([per_row(np.cos(ar), np.cos(ar)), per_col(np.cos(ac), np.cos(ac))], -1)
        sin = jnp.concatenate([per_row(-np.sin(ar), np.sin(ar)), per_col(-np.sin(ac), np.sin(ac))], -1)
        reps = width // (4 * n)
        cos, sin = jnp.tile(cos, (1, reps)), jnp.tile(sin, (1, reps))
        cos = jnp.concatenate([cos, jnp.ones((ST - S, cos.shape[1]), F32)], 0)
        sin = jnp.concatenate([sin, jnp.zeros((ST - S, sin.shape[1]), F32)], 0)
        return cos, sin

    ckr, skr = tables(ROPE // 4, ROPE)
    pad = jnp.zeros((ST, LANES - ROPE), F32)
    ckr, skr = jnp.concatenate([ckr, pad], -1), jnp.concatenate([skr, pad], -1)
    cd, sd = tables(DIFF_DQK // 4, 128)
    return ckr, skr, cd, sd


def _prepare_weights(tabs, w_in, mla_gq, mla_wuq, mla_gkv, mla_wukv, diff_lq1, diff_lk1, diff_lq2, diff_lk2,
                     diff_subln_g, sgu_ln_g, sgu_ln_b, sgu_ws, sgu_bs, w_o, ln1_g, ln1_b, ffn_wup, ffn_convw,
                     ffn_convb, ffn_wdown, ln2_g, ln2_b):
    w_in = w_in.astype(BF16)
    split = Q_LORA + KV_LORA + ROPE
    zpad = jnp.zeros((DEPTH, D, LANES - ROPE), BF16)
    win_wide = jnp.concatenate([w_in[:, :, :split], zpad, w_in[:, :, split:]], axis=2)

    wuq = mla_wuq.astype(BF16).reshape(DEPTH, Q_LORA, HEADS, NOPE + ROPE)
    zq = jnp.zeros((DEPTH, Q_LORA, HEADS, LANES - ROPE), BF16)
    wuq_main = jnp.concatenate([wuq, zq], -1).reshape(DEPTH, Q_LORA, HEADS * QK)
    wukv = mla_wukv.reshape(DEPTH, KV_LORA, HEADS, NOPE + DV)
    ckr, skr, cd, sd = tabs
    row = lambda v: v[:, None, :]
    bias = jnp.repeat(jnp.swapaxes(sgu_bs, 1, 2), CHUNK_W // GROUPS, axis=2)
    return dict(
        win=win_wide, gq=row(mla_gq), wuq=wuq_main, gkv=row(mla_gkv),
        wk=wukv[..., :NOPE].reshape(DEPTH, KV_LORA, HEADS * NOPE).astype(BF16),
        wv=wukv[..., NOPE:].reshape(DEPTH, KV_LORA, HEADS * DV).astype(BF16),
        ckr=ckr, skr=skr, cd=cd, sd=sd, lng=row(sgu_ln_g), lnb=row(sgu_ln_b),
        ws=sgu_ws.reshape(DEPTH, GROUPS * CHUNK, CHUNK).astype(BF16), bs=bias,
        lqk=jnp.stack([diff_lq1, diff_lk1, diff_lq2, diff_lk2], axis=1),
        subg=row(jnp.tile(diff_subln_g, (1, HEADS))),
        wo=w_o.astype(BF16), ln1g=row(ln1_g), ln1b=row(ln1_b),
        wup=ffn_wup.astype(BF16), convw=ffn_convw, convb=row(ffn_convb),
        wdn=ffn_wdown.astype(BF16), ln2g=row(ln2_g), ln2b=row(ln2_b))


def kernel(x, c, ctx, c_ctx, ada_w, ada_b, w_in, mla_gq, mla_wuq, mla_gkv, mla_wukv, diff_lq1, diff_lk1, diff_lq2, diff_lk2, diff_subln_g, sgu_ln_g, sgu_ln_b, sgu_ws, sgu_bs, w_o, ln1_g, ln1_b, ffn_wup, ffn_convw, ffn_convb, ffn_wdown, ln2_g, ln2_b):
    nb = x.shape[0]
    cvec = jnp.concatenate([c, c_ctx[None], jnp.zeros((MOD_ROWS - nb - 1, D), F32)], 0)
    mod = _modulation(cvec, ada_w, ada_b)
    tabs = _rope_tables()
    w = _prepare_weights(tabs, w_in, mla_gq, mla_wuq, mla_gkv, mla_wukv, diff_lq1, diff_lk1, diff_lq2, diff_lk2,
                         diff_subln_g, sgu_ln_g, sgu_ln_b, sgu_ws, sgu_bs, w_o, ln1_g, ln1_b, ffn_wup, ffn_convw,
                         ffn_convb, ffn_wdown, ln2_g, ln2_b)
    xs, xc = x, ctx
    for l in range(DEPTH):
        n_tiles = NT if l < DEPTH - 1 else NT_LAT
        lam_init = 0.8 - 0.6 * math.exp(-0.3 * l)
        qm, km, vmt, qd, kd, vdt, cm = _inproj(l, xs, xc, mod, w)
        om = _mla_attention(qm, km, vmt, n_tiles)
        od = _diff_attention(l, qd, kd, vdt, w["lqk"], w["subg"], lam_init, n_tiles)
        x1 = _outproj(l, xs, xc, om, od, cm, mod, w, n_tiles)
        xs = xc = _ffn(l, x1, mod, w, n_tiles)
    return xs
```

```python
import functools
import math

import jax
import jax.numpy as jnp
import numpy as np
from jax import lax
from jax.experimental import pallas as pl
from jax.experimental.pallas import tpu as pltpu

F32 = jnp.float32
BF16 = jnp.bfloat16

D = 1024
S = 8192
C = 256
GRID_W = 64
DEPTH = 2
T = 512
ST = S + 2 * C
NT = ST // T
NT_LAT = S // T
HEADS = 4
Q_LORA = 384
KV_LORA = 256
NOPE = 128
ROPE = 64
DV = 128
DIFF_W = 256
DIFF_DQK = 32
DIFF_DV = 64
CHUNK_W = 256
CHUNK = 128
GROUPS = 4
D_FF = 2816
FF_CHUNK = 256
HALO = 16
LN_PARTS = 2
ONES_ROWS = 16
MLA_VROWS = DV + ONES_ROWS
DIFF_VROWS = DIFF_DV + ONES_ROWS
EPS = 1e-6
ROPE_BASE = 10000.0
DN_ALPHA = (2 * DEPTH) ** 0.25
MLA_SCALE = (NOPE + ROPE) ** -0.5
DIFF_SCALE = DIFF_DQK ** -0.5
LOG2E = math.log2(math.e)
MLA_QSCALE = MLA_SCALE * LOG2E
DIFF_QSCALE = DIFF_SCALE * LOG2E
LANES = 128
QK = NOPE + LANES
MLA_W = HEADS * DV
Z_KV = Q_LORA
Z_KR = Z_KV + KV_LORA
Z_DQ = Z_KR + LANES
Z_DK = Z_DQ + DIFF_W
Z_DV = Z_DK + DIFF_W
Z_CH = Z_DV + DIFF_W
ZW = Z_CH + 2 * CHUNK_W
MOD_ROWS = 8
CTX_ROW = 2
VMEM_LIMIT = 56 * 1024 * 1024

_NT_DIMS = (((1,), (1,)), ((), ()))


def _dot(a, b):
    return jnp.dot(a, b, preferred_element_type=F32)


def _rms(x, g):
    return (x * lax.rsqrt(jnp.mean(x * x, axis=-1, keepdims=True) + EPS)) * g


def _ln(x, g, b):
    mu = jnp.mean(x, axis=-1, keepdims=True)
    xc = x - mu
    var = jnp.mean(xc * xc, axis=-1, keepdims=True)
    return (xc * lax.rsqrt(var + EPS)) * g + b


def _params(n_axes):
    return pltpu.CompilerParams(dimension_semantics=("arbitrary",) * n_axes,
                                vmem_limit_bytes=VMEM_LIMIT)


def _layer_spec(l, shape):
    nd = len(shape)
    return pl.BlockSpec((None,) + shape, lambda *_: (l,) + (0,) * nd, pipeline_mode=pl.Buffered(1))


MOD_BLOCK = 1536


def _mod_kernel(c_ref, w_ref, b_ref, o_ref):
    c = c_ref[...]
    sc = (c * (1.0 / (1.0 + jnp.exp(-c)))).astype(BF16)
    o_ref[...] = _dot(sc, w_ref[...].astype(BF16)) + b_ref[...]


def _modulation(cvec, ada_w, ada_b):
    return pl.pallas_call(
        _mod_kernel,
        out_shape=jax.ShapeDtypeStruct((DEPTH, MOD_ROWS, 6 * D), F32),
        grid=(DEPTH, 6 * D // MOD_BLOCK),
        in_specs=[pl.BlockSpec((MOD_ROWS, D), lambda l, j: (0, 0)),
                  pl.BlockSpec((None, D, MOD_BLOCK), lambda l, j: (l, 0, j)),
                  pl.BlockSpec((None, 1, MOD_BLOCK), lambda l, j: (l, 0, j))],
        out_specs=pl.BlockSpec((None, MOD_ROWS, MOD_BLOCK), lambda l, j: (l, 0, j)),
        compiler_params=_params(2),
        name="modulation",
    )(cvec, ada_w, ada_b.reshape(DEPTH, 1, 6 * D))


def _stream_tile(x_ref, c_ref, tile):
    return jnp.concatenate([_stream_half(x_ref, c_ref, tile, half) for half in range(T // C)], axis=0)


def _stream_half(x_ref, c_ref, tile, half):
    context = c_ref[...] if half == 0 else jnp.broadcast_to(c_ref[C - 1:C, :], (C, D))
    return jnp.where(tile == NT_LAT, context, x_ref[half * C:(half + 1) * C, :])


def _stream_specs(x, c):
    ctx_block = 0 if c.shape[1] == C else S // C
    return [pl.BlockSpec((None, T, D), lambda b, i: (b, jnp.minimum(i, NT_LAT - 1), 0)),
            pl.BlockSpec((None, C, D), lambda b, i: (b, ctx_block, 0))]


def _mod_row(tile):
    return jnp.where(tile == NT_LAT, CTX_ROW, pl.program_id(0))


def _swap_lane_halves(x, half):
    lane = lax.broadcasted_iota(jnp.int32, x.shape, 1)
    return jnp.where(lane % (2 * half) < half, pltpu.roll(x, LANES - half, 1), pltpu.roll(x, half, 1))


def _rotate(x, cos, sin_signed, half):
    return x * cos + _swap_lane_halves(x, half) * sin_signed


def _inproj_kernel(x_ref, c_ref, mod_ref, win_ref, gq_ref, wuq_ref, gkv_ref, wk_ref, wv_ref,
                   ckr_ref, skr_ref, cd_ref, sd_ref, lng_ref, lnb_ref, ws_ref, bs_ref,
                   qmt_ref, km_ref, vmt_ref, qdt_ref, kd_ref, vdt_ref, cm_ref):
    r = _mod_row(pl.program_id(1))
    sh1 = mod_ref[pl.ds(r, 1), 0:D]
    sc1 = mod_ref[pl.ds(r, 1), D:2 * D]
    group = lax.broadcasted_iota(jnp.int32, (CHUNK, CHUNK_W), 1) // (CHUNK_W // GROUPS)
    ones = jnp.ones((ONES_ROWS, T), BF16)
    h = (_stream_tile(x_ref, c_ref, pl.program_id(1)) * (1.0 + sc1) + sh1).astype(BF16)
    z = _dot(h, win_ref[...])
    positioned = pl.program_id(1) < NT_LAT
    kr_pad = jnp.zeros((T, LANES - ROPE), F32)
    ckr = jnp.concatenate([jnp.where(positioned, ckr_ref[...], 1.0), kr_pad], axis=1)
    skr = jnp.concatenate([jnp.where(positioned, skr_ref[...], 0.0), kr_pad], axis=1)
    cd = jnp.tile(jnp.where(positioned, cd_ref[...], 1.0), (1, MAPS_PER_BLOCK))
    sd = jnp.tile(jnp.where(positioned, sd_ref[...], 0.0), (1, MAPS_PER_BLOCK))

    qn = _rms(z[:, 0:Q_LORA], gq_ref[...]).astype(BF16)
    qa = _dot(qn, wuq_ref[...])
    for hd in range(HEADS):
        nope = qa[:, hd * QK:hd * QK + NOPE]
        rot = _rotate(qa[:, hd * QK + NOPE:(hd + 1) * QK], ckr, skr, ROPE // 4)
        qmt_ref[hd * QK:hd * QK + NOPE, :] = (nope * MLA_QSCALE).T.astype(BF16)
        qmt_ref[hd * QK + NOPE:(hd + 1) * QK, :] = (rot * MLA_QSCALE).T.astype(BF16)

    kvn = _rms(z[:, Q_LORA:Q_LORA + KV_LORA], gkv_ref[...]).astype(BF16)
    kn = _dot(kvn, wk_ref[...])
    vt = _dot(kvn, wv_ref[...]).T.astype(BF16)
    for hd in range(HEADS):
        vmt_ref[hd * MLA_VROWS:hd * MLA_VROWS + DV, :] = vt[hd * DV:(hd + 1) * DV]
        vmt_ref[hd * MLA_VROWS + DV:(hd + 1) * MLA_VROWS, :] = ones
    kr = _rotate(z[:, Z_KR:Z_DQ], ckr, skr, ROPE // 4).astype(BF16)
    for hd in range(HEADS):
        km_ref[:, hd * QK:hd * QK + NOPE] = kn[:, hd * NOPE:(hd + 1) * NOPE].astype(BF16)
        km_ref[:, hd * QK + NOPE:(hd + 1) * QK] = kr

    for lo in range(0, DIFF_W, LANES):
        qdt_ref[lo:lo + LANES, :] = (_rotate(z[:, Z_DQ + lo:Z_DQ + lo + LANES], cd, sd, DIFF_DQK // 4)
                                     * DIFF_QSCALE).T.astype(BF16)
        kd_ref[:, lo:lo + LANES] = _rotate(z[:, Z_DK + lo:Z_DK + lo + LANES], cd, sd, DIFF_DQK // 4).astype(BF16)
    vt = z[:, Z_DV:Z_CH].T.astype(BF16)
    for hd in range(HEADS):
        vdt_ref[hd * DIFF_VROWS:hd * DIFF_VROWS + DIFF_DV, :] = vt[hd * DIFF_DV:(hd + 1) * DIFF_DV]
        vdt_ref[hd * DIFF_VROWS + DIFF_DV:(hd + 1) * DIFF_VROWS, :] = ones

    zc = z[:, Z_CH:ZW]
    zc = 0.5 * zc * (1.0 + jnp.tanh(math.sqrt(2.0 / math.pi) * (zc + 0.044715 * (zc * zc * zc))))
    u = zc[:, :CHUNK_W]
    v = _ln(zc[:, CHUNK_W:], lng_ref[...], lnb_ref[...]).astype(BF16)
    for n in range(T // CHUNK):
        mixed_all = _dot(ws_ref[...], v[n * CHUNK:(n + 1) * CHUNK, :])
        mixed = mixed_all[0:CHUNK]
        for g in range(1, GROUPS):
            mixed = jnp.where(group == g, mixed_all[g * CHUNK:(g + 1) * CHUNK], mixed)
        cm_ref[n * CHUNK:(n + 1) * CHUNK, :] = (u[n * CHUNK:(n + 1) * CHUNK, :]
                                                * (mixed + bs_ref[...])).astype(BF16)


def _inproj(l, x, c, mod, w):
    nb = x.shape[0]
    row = lambda width: pl.BlockSpec((None, T, width), lambda b, i: (b, i, 0))
    tab = lambda width: pl.BlockSpec((T, width), lambda b, i: (jnp.minimum(i, NT_LAT - 1), 0))
    rows = lambda width: jax.ShapeDtypeStruct((nb, ST, width), BF16)
    cols = lambda width: jax.ShapeDtypeStruct((nb, NT, width, T), BF16)
    col = lambda width: pl.BlockSpec((None, None, width, T), lambda b, i: (b, i, 0, 0))
    outs = [cols(HEADS * QK), rows(HEADS * QK), cols(HEADS * MLA_VROWS), cols(DIFF_W), rows(DIFF_W),
            cols(HEADS * DIFF_VROWS), rows(CHUNK_W)]
    return pl.pallas_call(
        _inproj_kernel,
        out_shape=outs,
        grid=(nb, NT),
        in_specs=_stream_specs(x, c) + [
                  _layer_spec(l, (MOD_ROWS, 6 * D)), _layer_spec(l, (D, ZW)),
                  _layer_spec(l, (1, Q_LORA)), _layer_spec(l, (Q_LORA, HEADS * QK)),
                  _layer_spec(l, (1, KV_LORA)), _layer_spec(l, (KV_LORA, HEADS * NOPE)), _layer_spec(l, (KV_LORA, HEADS * DV)),
                  tab(ROPE), tab(ROPE), tab(DIFF_DQK), tab(DIFF_DQK),
                  _layer_spec(l, (1, CHUNK_W)), _layer_spec(l, (1, CHUNK_W)),
                  _layer_spec(l, (GROUPS * CHUNK, CHUNK)), _layer_spec(l, (CHUNK, CHUNK_W))],
        out_specs=[col(HEADS * QK), row(HEADS * QK), col(HEADS * MLA_VROWS), col(DIFF_W), row(DIFF_W),
                   col(HEADS * DIFF_VROWS), row(CHUNK_W)],
        compiler_params=_params(2),
        name="inproj",
    )(x, c, mod, w["win"], w["gq"], w["wuq"], w["gkv"], w["wk"], w["wv"],
      w["ckr"], w["skr"], w["cd"], w["sd"], w["lng"], w["lnb"], w["ws"], w["bs"])


def _key_rows(c, keys=T):
    if isinstance(c, int):
        return pl.ds(c * T, keys)
    return pl.ds(pl.multiple_of(c * T, T), keys)


def _attend(n, qk, qk_next_tile, vt, s_ref, cmax_ref, m_ref, acc_ref, tile):
    m_ref[...] = jnp.full(m_ref.shape, -jnp.inf, F32)
    acc_ref[...] = jnp.zeros(acc_ref.shape, F32)

    def scores(j, c, qk=qk, keys=T):
        s = qk(j, c, keys)
        s_ref[j, 0:keys, :] = s
        cmax_ref[j] = jnp.max(s, axis=0, keepdims=True)

    def update(j, c, keys=T):
        m_prev = m_ref[j]
        m_new = jnp.maximum(m_prev, cmax_ref[j])
        alpha = jnp.exp2(m_prev - m_new)
        p = jnp.exp2(s_ref[j, 0:keys, :] - m_new).astype(BF16)
        acc_ref[j] = alpha * acc_ref[j] + _dot(vt(j, c, keys), p)
        m_ref[j] = m_new

    @pl.when(tile == 0)
    def _():
        for j in range(n - 1):
            scores(j, 0)

    @pl.when(tile < NT_LAT)
    def _():
        def body(c, carry):
            scores(n - 1, c)
            for j in range(n - 1):
                update(j, c)
                scores(j, c + 1)
            update(n - 1, c)
            return carry

        lax.fori_loop(0, NT_LAT, body, 0, unroll=4)

    scores(n - 1, NT_LAT, keys=C)
    first_chunk_of_next = jnp.where(tile + 1 == NT_LAT, NT_LAT, 0)
    for j in range(n):
        update(j, NT_LAT, keys=C)
        if j < n - 1:
            scores(j, first_chunk_of_next, qk_next_tile)


def _next_tile_spec(block, n_q_tiles):
    return pl.BlockSpec(block, lambda b, i: (b, jnp.minimum(i + 1, n_q_tiles - 1), 0, 0))


def _mla_attn_kernel(qt_ref, qt_next_ref, k_ref, vt_ref, o_ref, s_ref, cmax_ref, m_ref, acc_ref):
    def scores_with(q_ref):
        return lambda j, c, keys: _dot(k_ref[_key_rows(c, keys), j * QK:(j + 1) * QK],
                                       q_ref[j * QK:(j + 1) * QK, :])

    _attend(HEADS, scores_with(qt_ref), scores_with(qt_next_ref),
            lambda j, c, keys: vt_ref[c, j * MLA_VROWS:(j + 1) * MLA_VROWS, 0:keys],
            s_ref, cmax_ref, m_ref, acc_ref, pl.program_id(1))
    for hd in range(HEADS):
        o = acc_ref[hd, 0:DV, :] / acc_ref[hd, DV:DV + 1, :]
        o_ref[:, hd * DV:(hd + 1) * DV] = o.T.astype(BF16)


def _mla_attention(qm, km, vmt, n_q_tiles):
    nb = qm.shape[0]
    resident = lambda shape: pl.BlockSpec((None,) + shape, lambda b, i: (b,) + (0,) * len(shape),
                                          pipeline_mode=pl.Buffered(1))
    return pl.pallas_call(
        _mla_attn_kernel,
        out_shape=jax.ShapeDtypeStruct((nb, n_q_tiles * T, HEADS * DV), BF16),
        grid=(nb, n_q_tiles),
        in_specs=[pl.BlockSpec((None, None, HEADS * QK, T), lambda b, i: (b, i, 0, 0)),
                  _next_tile_spec((None, None, HEADS * QK, T), n_q_tiles),
                  resident((ST, HEADS * QK)), resident((NT, HEADS * MLA_VROWS, T))],
        out_specs=pl.BlockSpec((None, T, HEADS * DV), lambda b, i: (b, i, 0)),
        scratch_shapes=[pltpu.VMEM((HEADS, T, T), F32),
                        pltpu.VMEM((HEADS, 1, T), F32), pltpu.VMEM((HEADS, 1, T), F32),
                        pltpu.VMEM((HEADS, MLA_VROWS, T), F32)],
        compiler_params=_params(2),
        name="mla_attention",
    )(qm, qm, km, vmt)


DIFF_MAPS = 2 * HEADS


MAPS_PER_BLOCK = LANES // DIFF_DQK


def _diff_attn_kernel(lam_init, qt_ref, qt_next_ref, k_ref, vt_ref, lqk_ref, g_ref, o_ref,
                      qm_ref, qm_next_ref, s_ref, cmax_ref, m_ref, acc_ref):
    dim = lax.broadcasted_iota(jnp.int32, (LANES, T), 0)
    for src_ref, maps_ref in ((qt_ref, qm_ref), (qt_next_ref, qm_next_ref)):
        for j in range(DIFF_MAPS):
            blk = j // MAPS_PER_BLOCK
            qt = src_ref[blk * LANES:(blk + 1) * LANES, :]
            maps_ref[j] = jnp.where(dim // DIFF_DQK == j % MAPS_PER_BLOCK, qt, jnp.zeros_like(qt))

    def scores_with(maps_ref):
        def qk(j, c, keys):
            blk = j // MAPS_PER_BLOCK
            return _dot(k_ref[_key_rows(c, keys), blk * LANES:(blk + 1) * LANES], maps_ref[j])
        return qk

    _attend(DIFF_MAPS, scores_with(qm_ref), scores_with(qm_next_ref),
            lambda j, c, keys: vt_ref[c, (j // 2) * DIFF_VROWS:(j // 2 + 1) * DIFF_VROWS, 0:keys],
            s_ref, cmax_ref, m_ref, acc_ref, pl.program_id(1))

    lqk = lqk_ref[...]
    lam = (jnp.exp(jnp.sum(lqk[0:1] * lqk[1:2], axis=-1, keepdims=True))
           - jnp.exp(jnp.sum(lqk[2:3] * lqk[3:4], axis=-1, keepdims=True)) + lam_init)
    attn = lambda j: acc_ref[j, 0:DIFF_DV, :] / acc_ref[j, DIFF_DV:DIFF_DV + 1, :]
    heads = []
    for hd in range(HEADS):
        d = attn(2 * hd) - lam * attn(2 * hd + 1)
        ms = jnp.mean(d * d, axis=0, keepdims=True)
        heads.append(d * lax.rsqrt(ms + EPS))
    out = jnp.concatenate(heads, axis=0).T
    o_ref[...] = ((out * g_ref[...]) * (1.0 - lam_init)).astype(BF16)


def _diff_attention(l, qd, kd, vdt, lqk, g_row, lam_init, n_q_tiles):
    nb = qd.shape[0]
    return pl.pallas_call(
        functools.partial(_diff_attn_kernel, lam_init),
        out_shape=jax.ShapeDtypeStruct((nb, n_q_tiles * T, DIFF_W), BF16),
        grid=(nb, n_q_tiles),
        in_specs=[pl.BlockSpec((None, None, DIFF_W, T), lambda b, i: (b, i, 0, 0)),
                  _next_tile_spec((None, None, DIFF_W, T), n_q_tiles),
                  pl.BlockSpec((None, ST, DIFF_W), lambda b, i: (b, 0, 0)),
                  pl.BlockSpec((None, NT, HEADS * DIFF_VROWS, T), lambda b, i: (b, 0, 0, 0)),
                  _layer_spec(l, (4, DIFF_DQK)), _layer_spec(l, (1, DIFF_W))],
        out_specs=pl.BlockSpec((None, T, DIFF_W), lambda b, i: (b, i, 0)),
        scratch_shapes=[pltpu.VMEM((DIFF_MAPS, LANES, T), BF16), pltpu.VMEM((DIFF_MAPS, LANES, T), BF16),
                        pltpu.VMEM((DIFF_MAPS, T, T), F32),
                        pltpu.VMEM((DIFF_MAPS, 1, T), F32), pltpu.VMEM((DIFF_MAPS, 1, T), F32),
                        pltpu.VMEM((DIFF_MAPS, DIFF_VROWS, T), F32)],
        compiler_params=_params(2),
        name="diff_attention",
    )(qd, qd, kd, vdt, lqk, g_row)


def _outproj_kernel(x_ref, c_ref, om_ref, od_ref, cm_ref, mod_ref, wo_ref, g_ref, b_ref, o_ref):
    r = _mod_row(pl.program_id(1))
    g1 = mod_ref[pl.ds(r, 1), 2 * D:3 * D]
    for half in range(T // C):
        rows = pl.ds(half * C, C)
        y = (_dot(om_ref[rows, :], wo_ref[0:MLA_W, :]) + _dot(od_ref[rows, :], wo_ref[MLA_W:MLA_W + DIFF_W, :])
             + _dot(cm_ref[rows, :], wo_ref[MLA_W + DIFF_W:D, :]))
        x = _stream_half(x_ref, c_ref, pl.program_id(1), half)
        o_ref[rows, :] = _ln(DN_ALPHA * x + g1 * y, g_ref[...], b_ref[...])


def _outproj(l, x, c, om, od, cm, mod, w, n_tiles):
    nb = x.shape[0]
    row = lambda width: pl.BlockSpec((None, T, width), lambda b, i: (b, i, 0))
    return pl.pallas_call(
        _outproj_kernel,
        out_shape=jax.ShapeDtypeStruct((nb, n_tiles * T, D), F32),
        grid=(nb, n_tiles),
        in_specs=_stream_specs(x, c) + [
                  row(MLA_W), row(DIFF_W), row(CHUNK_W), _layer_spec(l, (MOD_ROWS, 6 * D)),
                  _layer_spec(l, (D, D)), _layer_spec(l, (1, D)), _layer_spec(l, (1, D))],
        out_specs=row(D),
        compiler_params=_params(2),
        name="outproj",
    )(x, c, om, od, cm, mod, w["wo"], w["ln1g"], w["ln1b"])


def _ffn_kernel(x_ref, xp_ref, xn_ref, mod_ref, wup_ref, cw_ref, cb_ref, wdn_ref, g_ref, b_ref,
                o_ref, h_ref, act_ref):
    i = pl.program_id(1)
    r = _mod_row(i)
    sh2 = mod_ref[pl.ds(r, 1), 3 * D:4 * D]
    sc2 = mod_ref[pl.ds(r, 1), 4 * D:5 * D]
    g2 = mod_ref[pl.ds(r, 1), 5 * D:6 * D]
    x = x_ref[...]
    modulate = lambda t: t * (1.0 + sc2) + sh2
    seq_start = jnp.logical_or(i == 0, i == NT_LAT)
    seq_end = i >= NT_LAT - 1
    h_ref[0:HALO, :] = jnp.where(seq_start, 0.0, modulate(xp_ref[...])).astype(BF16)
    hm = modulate(x)
    row = lax.broadcasted_iota(jnp.int32, (T, 1), 0)
    hm = jnp.where(jnp.logical_and(i == NT_LAT, row >= C), 0.0, hm)
    h_ref[HALO:HALO + T, :] = hm.astype(BF16)
    h_ref[HALO + T:, :] = jnp.where(seq_end, 0.0, modulate(xn_ref[...])).astype(BF16)
    h = h_ref[...]

    def conv(col):
        up = _dot(h, wup_ref[:, col:col + FF_CHUNK])
        cw = cw_ref[:, col:col + FF_CHUNK]
        prev = pltpu.roll(up, 1, 0)[HALO:HALO + T]
        nxt = pltpu.roll(up, T + 2 * HALO - 1, 0)[HALO:HALO + T]
        return prev * cw[0:1] + up[HALO:HALO + T] * cw[1:2] + nxt * cw[2:3] + cb_ref[:, col:col + FF_CHUNK]

    for j in range(D_FF // FF_CHUNK):
        gate = conv(j * FF_CHUNK)
        val = conv(D_FF + j * FF_CHUNK)
        act_ref[:, j * FF_CHUNK:(j + 1) * FF_CHUNK] = ((gate * (1.0 / (1.0 + jnp.exp(-gate)))) * val).astype(BF16)
    for part in range(LN_PARTS):
        rows = pl.ds(part * (T // LN_PARTS), T // LN_PARTS)
        y = _dot(act_ref[rows, :], wdn_ref[...])
        o_ref[rows, :] = _ln(DN_ALPHA * x_ref[rows, :] + g2 * y, g_ref[...], b_ref[...])


def _ffn(l, x1, mod, w, n_tiles):
    nb, rows, _ = x1.shape
    nh = rows // HALO
    per = T // HALO
    return pl.pallas_call(
        _ffn_kernel,
        out_shape=jax.ShapeDtypeStruct((nb, n_tiles * T, D), F32),
        grid=(nb, n_tiles),
        in_specs=[pl.BlockSpec((None, T, D), lambda b, i: (b, i, 0)),
                  pl.BlockSpec((None, HALO, D), lambda b, i: (b, jnp.maximum(i * per - 1, 0), 0)),
                  pl.BlockSpec((None, HALO, D), lambda b, i: (b, jnp.minimum((i + 1) * per, nh - 1), 0)),
                  _layer_spec(l, (MOD_ROWS, 6 * D)), _layer_spec(l, (D, 2 * D_FF)),
                  _layer_spec(l, (3, 2 * D_FF)), _layer_spec(l, (1, 2 * D_FF)), _layer_spec(l, (D_FF, D)),
                  _layer_spec(l, (1, D)), _layer_spec(l, (1, D))],
        out_specs=pl.BlockSpec((None, T, D), lambda b, i: (b, i, 0)),
        scratch_shapes=[pltpu.VMEM((T + 2 * HALO, D), BF16), pltpu.VMEM((T, D_FF), BF16)],
        compiler_params=_params(2),
        name="conv_ffn",
    )(x1, x1, x1, mod, w["wup"], w["convw"], w["convb"], w["wdn"], w["ln2g"], w["ln2b"])


def _rope_tables():
    grid_h = S // GRID_W

    def tables(n):
        inv = ROPE_BASE ** (-np.arange(n, dtype=np.float64) / n)
        ar = np.repeat(np.arange(grid_h, dtype=np.float64)[:, None] * inv[None, :], GRID_W, axis=0)
        ac = np.tile(np.arange(GRID_W, dtype=np.float64)[:, None] * inv[None, :], (grid_h, 1))
        cos = np.concatenate([np.cos(ar), np.cos(ar), np.cos(ac), np.cos(ac)], -1)
        sin = np.concatenate([-np.sin(ar), np.sin(ar), -np.sin(ac), np.sin(ac)], -1)
        return jnp.asarray(cos, F32), jnp.asarray(sin, F32)

    return tables(ROPE // 4) + tables(DIFF_DQK // 4)


def _prepare_weights(tabs, w_in, mla_gq, mla_wuq, mla_gkv, mla_wukv, diff_lq1, diff_lk1, diff_lq2, diff_lk2,
                     diff_subln_g, sgu_ln_g, sgu_ln_b, sgu_ws, sgu_bs, w_o, ln1_g, ln1_b, ffn_wup, ffn_convw,
                     ffn_convb, ffn_wdown, ln2_g, ln2_b):
    w_in = w_in.astype(BF16)
    split = Q_LORA + KV_LORA + ROPE
    zpad = jnp.zeros((DEPTH, D, LANES - ROPE), BF16)
    win_wide = jnp.concatenate([w_in[:, :, :split], zpad, w_in[:, :, split:]], axis=2)

    wuq = mla_wuq.astype(BF16).reshape(DEPTH, Q_LORA, HEADS, NOPE + ROPE)
    zq = jnp.zeros((DEPTH, Q_LORA, HEADS, LANES - ROPE), BF16)
    wuq_main = jnp.concatenate([wuq, zq], -1).reshape(DEPTH, Q_LORA, HEADS * QK)
    wukv = mla_wukv.reshape(DEPTH, KV_LORA, HEADS, NOPE + DV)
    ckr, skr, cd, sd = tabs
    row = lambda v: v[:, None, :]
    bias = jnp.repeat(jnp.swapaxes(sgu_bs, 1, 2), CHUNK_W // GROUPS, axis=2)
    return dict(
        win=win_wide, gq=row(mla_gq), wuq=wuq_main, gkv=row(mla_gkv),
        wk=wukv[..., :NOPE].reshape(DEPTH, KV_LORA, HEADS * NOPE).astype(BF16),
        wv=wukv[..., NOPE:].reshape(DEPTH, KV_LORA, HEADS * DV).astype(BF16),
        ckr=ckr, skr=skr, cd=cd, sd=sd, lng=row(sgu_ln_g), lnb=row(sgu_ln_b),
        ws=sgu_ws.reshape(DEPTH, GROUPS * CHUNK, CHUNK).astype(BF16), bs=bias,
        lqk=jnp.stack([diff_lq1, diff_lk1, diff_lq2, diff_lk2], axis=1),
        subg=row(jnp.tile(diff_subln_g, (1, HEADS))),
        wo=w_o.astype(BF16), ln1g=row(ln1_g), ln1b=row(ln1_b),
        wup=ffn_wup.astype(BF16), convw=ffn_convw, convb=row(ffn_convb),
        wdn=ffn_wdown.astype(BF16), ln2g=row(ln2_g), ln2b=row(ln2_b))


def kernel(x, c, ctx, c_ctx, ada_w, ada_b, w_in, mla_gq, mla_wuq, mla_gkv, mla_wukv, diff_lq1, diff_lk1, diff_lq2, diff_lk2, diff_subln_g, sgu_ln_g, sgu_ln_b, sgu_ws, sgu_bs, w_o, ln1_g, ln1_b, ffn_wup, ffn_convw, ffn_convb, ffn_wdown, ln2_g, ln2_b):
    nb = x.shape[0]
    cvec = jnp.concatenate([c, c_ctx[None], jnp.zeros((MOD_ROWS - nb - 1, D), F32)], 0)
    mod = _modulation(cvec, ada_w, ada_b)
    tabs = _rope_tables()
    w = _prepare_weights(tabs, w_in, mla_gq, mla_wuq, mla_gkv, mla_wukv, diff_lq1, diff_lk1, diff_lq2, diff_lk2,
                         diff_subln_g, sgu_ln_g, sgu_ln_b, sgu_ws, sgu_bs, w_o, ln1_g, ln1_b, ffn_wup, ffn_convw,
                         ffn_convb, ffn_wdown, ln2_g, ln2_b)
    xs, xc = x, ctx
    for l in range(DEPTH):
        n_tiles = NT if l < DEPTH - 1 else NT_LAT
        lam_init = 0.8 - 0.6 * math.exp(-0.3 * l)
        qm, km, vmt, qd, kd, vdt, cm = _inproj(l, xs, xc, mod, w)
        om = _mla_attention(qm, km, vmt, n_tiles)
        od = _diff_attention(l, qd, kd, vdt, w["lqk"], w["subg"], lam_init, n_tiles)
        x1 = _outproj(l, xs, xc, om, od, cm, mod, w, n_tiles)
        xs = xc = _ffn(l, x1, mod, w, n_tiles)
    return xs
```

```python
import functools
import math

import jax
import jax.numpy as jnp
import numpy as np
from jax import lax
from jax.experimental import pallas as pl
from jax.experimental.pallas import tpu as pltpu

F32 = jnp.float32
BF16 = jnp.bfloat16

D = 1024
S = 8192
C = 256
GRID_W = 64
DEPTH = 2
T = 512
ST = S + 2 * C
NT = ST // T
NT_LAT = S // T
HEADS = 4
Q_LORA = 384
KV_LORA = 256
NOPE = 128
ROPE = 64
DV = 128
DIFF_W = 256
DIFF_DQK = 32
DIFF_DV = 64
CHUNK_W = 256
CHUNK = 128
GROUPS = 4
D_FF = 2816
FF_CHUNK = 256
HALO = 16
LN_PARTS = 2
ONES_ROWS = 16
MLA_VROWS = DV + ONES_ROWS
DIFF_VROWS = DIFF_DV + ONES_ROWS
EPS = 1e-6
ROPE_BASE = 10000.0
DN_ALPHA = (2 * DEPTH) ** 0.25
MLA_SCALE = (NOPE + ROPE) ** -0.5
DIFF_SCALE = DIFF_DQK ** -0.5
LOG2E = math.log2(math.e)
MLA_QSCALE = MLA_SCALE * LOG2E
DIFF_QSCALE = DIFF_SCALE * LOG2E
LANES = 128
QK = NOPE + LANES
MLA_W = HEADS * DV
Z_KV = Q_LORA
Z_KR = Z_KV + KV_LORA
Z_DQ = Z_KR + LANES
Z_DK = Z_DQ + DIFF_W
Z_DV = Z_DK + DIFF_W
Z_CH = Z_DV + DIFF_W
ZW = Z_CH + 2 * CHUNK_W
MOD_ROWS = 8
CTX_ROW = 2
VMEM_LIMIT = 56 * 1024 * 1024

_NT_DIMS = (((1,), (1,)), ((), ()))


def _dot(a, b):
    return jnp.dot(a, b, preferred_element_type=F32)


def _rms(x, g):
    return (x * lax.rsqrt(jnp.mean(x * x, axis=-1, keepdims=True) + EPS)) * g


def _ln(x, g, b):
    mu = jnp.mean(x, axis=-1, keepdims=True)
    xc = x - mu
    var = jnp.mean(xc * xc, axis=-1, keepdims=True)
    return (xc * lax.rsqrt(var + EPS)) * g + b


def _params(n_axes):
    return pltpu.CompilerParams(dimension_semantics=("arbitrary",) * n_axes,
                                vmem_limit_bytes=VMEM_LIMIT)


def _layer_spec(l, shape):
    nd = len(shape)
    return pl.BlockSpec((None,) + shape, lambda *_: (l,) + (0,) * nd, pipeline_mode=pl.Buffered(1))


MOD_BLOCK = 1536


def _mod_kernel(c_ref, w_ref, b_ref, o_ref):
    c = c_ref[...]
    sc = (c * (1.0 / (1.0 + jnp.exp(-c)))).astype(BF16)
    o_ref[...] = _dot(sc, w_ref[...].astype(BF16)) + b_ref[...]


def _modulation(cvec, ada_w, ada_b):
    return pl.pallas_call(
        _mod_kernel,
        out_shape=jax.ShapeDtypeStruct((DEPTH, MOD_ROWS, 6 * D), F32),
        grid=(DEPTH, 6 * D // MOD_BLOCK),
        in_specs=[pl.BlockSpec((MOD_ROWS, D), lambda l, j: (0, 0)),
                  pl.BlockSpec((None, D, MOD_BLOCK), lambda l, j: (l, 0, j)),
                  pl.BlockSpec((None, 1, MOD_BLOCK), lambda l, j: (l, 0, j))],
        out_specs=pl.BlockSpec((None, MOD_ROWS, MOD_BLOCK), lambda l, j: (l, 0, j)),
        compiler_params=_params(2),
        name="modulation",
    )(cvec, ada_w, ada_b.reshape(DEPTH, 1, 6 * D))


def _stream_tile(x_ref, c_ref, tile):
    return jnp.concatenate([_stream_half(x_ref, c_ref, tile, half) for half in range(T // C)], axis=0)


def _stream_half(x_ref, c_ref, tile, half):
    context = c_ref[...] if half == 0 else jnp.broadcast_to(c_ref[C - 1:C, :], (C, D))
    return jnp.where(tile == NT_LAT, context, x_ref[half * C:(half + 1) * C, :])


def _stream_specs(x, c):
    ctx_block = 0 if c.shape[1] == C else S // C
    return [pl.BlockSpec((None, T, D), lambda b, i: (b, jnp.minimum(i, NT_LAT - 1), 0)),
            pl.BlockSpec((None, C, D), lambda b, i: (b, ctx_block, 0))]


def _mod_row(tile):
    return jnp.where(tile == NT_LAT, CTX_ROW, pl.program_id(0))


def _swap_lane_halves(x, half):
    lane = lax.broadcasted_iota(jnp.int32, x.shape, 1)
    return jnp.where(lane % (2 * half) < half, pltpu.roll(x, LANES - half, 1), pltpu.roll(x, half, 1))


def _rotate(x, cos, sin_signed, half):
    return x * cos + _swap_lane_halves(x, half) * sin_signed


def _inproj_kernel(x_ref, c_ref, mod_ref, win_ref, gq_ref, wuq_ref, gkv_ref, wkv_ref,
                   ckr_ref, skr_ref, cd_ref, sd_ref, lng_ref, lnb_ref, ws_ref, bs_ref,
                   qmt_ref, km_ref, vmt_ref, qdt_ref, kd_ref, vdt_ref, cm_ref):
    r = _mod_row(pl.program_id(1))
    sh1 = mod_ref[pl.ds(r, 1), 0:D]
    sc1 = mod_ref[pl.ds(r, 1), D:2 * D]
    group = lax.broadcasted_iota(jnp.int32, (CHUNK, CHUNK_W), 1) // (CHUNK_W // GROUPS)
    ones = jnp.ones((ONES_ROWS, T), BF16)
    h = (_stream_tile(x_ref, c_ref, pl.program_id(1)) * (1.0 + sc1) + sh1).astype(BF16)
    z = _dot(h, win_ref[:, 0:Z_DQ])
    positioned = pl.program_id(1) < NT_LAT
    kr_pad = jnp.zeros((T, LANES - ROPE), F32)
    ckr = jnp.concatenate([jnp.where(positioned, ckr_ref[...], 1.0), kr_pad], axis=1)
    skr = jnp.concatenate([jnp.where(positioned, skr_ref[...], 0.0), kr_pad], axis=1)
    cd = jnp.tile(jnp.where(positioned, cd_ref[...], 1.0), (1, MAPS_PER_BLOCK))
    sd = jnp.tile(jnp.where(positioned, sd_ref[...], 0.0), (1, MAPS_PER_BLOCK))

    qn = _rms(z[:, 0:Q_LORA], gq_ref[...]).astype(BF16)
    qa = _dot(qn, wuq_ref[...])
    for hd in range(HEADS):
        nope = qa[:, hd * QK:hd * QK + NOPE]
        rot = _rotate(qa[:, hd * QK + NOPE:(hd + 1) * QK], ckr, skr, ROPE // 4)
        qmt_ref[hd * QK:hd * QK + NOPE, :] = (nope * MLA_QSCALE).T.astype(BF16)
        qmt_ref[hd * QK + NOPE:(hd + 1) * QK, :] = (rot * MLA_QSCALE).T.astype(BF16)

    kvn = _rms(z[:, Q_LORA:Q_LORA + KV_LORA], gkv_ref[...]).astype(BF16)
    kv = _dot(kvn, wkv_ref[...])
    kr = _rotate(z[:, Z_KR:Z_DQ], ckr, skr, ROPE // 4).astype(BF16)
    for hd in range(HEADS):
        lo = hd * (NOPE + DV)
        km_ref[:, hd * QK:hd * QK + NOPE] = kv[:, lo:lo + NOPE].astype(BF16)
        km_ref[:, hd * QK + NOPE:(hd + 1) * QK] = kr
        vmt_ref[hd * MLA_VROWS:hd * MLA_VROWS + DV, :] = kv[:, lo + NOPE:lo + NOPE + DV].T.astype(BF16)
        vmt_ref[hd * MLA_VROWS + DV:(hd + 1) * MLA_VROWS, :] = ones

    zd = _dot(h, win_ref[:, Z_DQ:Z_CH])
    for lo in range(0, DIFF_W, LANES):
        qdt_ref[lo:lo + LANES, :] = (_rotate(zd[:, lo:lo + LANES], cd, sd, DIFF_DQK // 4)
                                     * DIFF_QSCALE).T.astype(BF16)
        kd_ref[:, lo:lo + LANES] = _rotate(zd[:, DIFF_W + lo:DIFF_W + lo + LANES], cd, sd,
                                           DIFF_DQK // 4).astype(BF16)
    vt = zd[:, 2 * DIFF_W:3 * DIFF_W].T.astype(BF16)
    for hd in range(HEADS):
        vdt_ref[hd * DIFF_VROWS:hd * DIFF_VROWS + DIFF_DV, :] = vt[hd * DIFF_DV:(hd + 1) * DIFF_DV]
        vdt_ref[hd * DIFF_VROWS + DIFF_DV:(hd + 1) * DIFF_VROWS, :] = ones

    zc = _dot(h, win_ref[:, Z_CH:ZW])
    zc = 0.5 * zc * (1.0 + jnp.tanh(math.sqrt(2.0 / math.pi) * (zc + 0.044715 * (zc * zc * zc))))
    u = zc[:, :CHUNK_W]
    v = _ln(zc[:, CHUNK_W:], lng_ref[...], lnb_ref[...]).astype(BF16)
    for n in range(T // CHUNK):
        mixed_all = _dot(ws_ref[...], v[n * CHUNK:(n + 1) * CHUNK, :])
        mixed = mixed_all[0:CHUNK]
        for g in range(1, GROUPS):
            mixed = jnp.where(group == g, mixed_all[g * CHUNK:(g + 1) * CHUNK], mixed)
        cm_ref[n * CHUNK:(n + 1) * CHUNK, :] = (u[n * CHUNK:(n + 1) * CHUNK, :]
                                                * (mixed + bs_ref[...])).astype(BF16)


def _inproj(l, x, c, mod, w):
    nb = x.shape[0]
    row = lambda width: pl.BlockSpec((None, T, width), lambda b, i: (b, i, 0))
    tab = lambda width: pl.BlockSpec((T, width), lambda b, i: (jnp.minimum(i, NT_LAT - 1), 0))
    rows = lambda width: jax.ShapeDtypeStruct((nb, ST, width), BF16)
    cols = lambda width: jax.ShapeDtypeStruct((nb, NT, width, T), BF16)
    col = lambda width: pl.BlockSpec((None, None, width, T), lambda b, i: (b, i, 0, 0))
    outs = [cols(HEADS * QK), rows(HEADS * QK), cols(HEADS * MLA_VROWS), cols(DIFF_W), rows(DIFF_W),
            cols(HEADS * DIFF_VROWS), rows(CHUNK_W)]
    return pl.pallas_call(
        _inproj_kernel,
        out_shape=outs,
        grid=(nb, NT),
        in_specs=_stream_specs(x, c) + [
                  _layer_spec(l, (MOD_ROWS, 6 * D)), _layer_spec(l, (D, ZW)),
                  _layer_spec(l, (1, Q_LORA)), _layer_spec(l, (Q_LORA, HEADS * QK)),
                  _layer_spec(l, (1, KV_LORA)), _layer_spec(l, (KV_LORA, HEADS * (NOPE + DV))),
                  tab(ROPE), tab(ROPE), tab(DIFF_DQK), tab(DIFF_DQK),
                  _layer_spec(l, (1, CHUNK_W)), _layer_spec(l, (1, CHUNK_W)),
                  _layer_spec(l, (GROUPS * CHUNK, CHUNK)), _layer_spec(l, (CHUNK, CHUNK_W))],
        out_specs=[col(HEADS * QK), row(HEADS * QK), col(HEADS * MLA_VROWS), col(DIFF_W), row(DIFF_W),
                   col(HEADS * DIFF_VROWS), row(CHUNK_W)],
        compiler_params=_params(2),
        name="inproj",
    )(x, c, mod, w["win"], w["gq"], w["wuq"], w["gkv"], w["wkv"],
      w["ckr"], w["skr"], w["cd"], w["sd"], w["lng"], w["lnb"], w["ws"], w["bs"])


def _key_rows(c, keys=T):
    if isinstance(c, int):
        return pl.ds(c * T, keys)
    return pl.ds(pl.multiple_of(c * T, T), keys)


def _attend(n, qk, qk_next_tile, vt, s_ref, cmax_ref, m_ref, acc_ref, tile):
    m_ref[...] = jnp.full(m_ref.shape, -jnp.inf, F32)
    acc_ref[...] = jnp.zeros(acc_ref.shape, F32)

    def scores(j, c, qk=qk, keys=T):
        s = qk(j, c, keys)
        s_ref[j, 0:keys, :] = s
        cmax_ref[j] = jnp.max(s, axis=0, keepdims=True)

    def update(j, c, keys=T):
        m_prev = m_ref[j]
        m_new = jnp.maximum(m_prev, cmax_ref[j])
        alpha = jnp.exp2(m_prev - m_new)
        p = jnp.exp2(s_ref[j, 0:keys, :] - m_new).astype(BF16)
        acc_ref[j] = alpha * acc_ref[j] + _dot(vt(j, c, keys), p)
        m_ref[j] = m_new

    @pl.when(tile == 0)
    def _():
        for j in range(n - 1):
            scores(j, 0)

    @pl.when(tile < NT_LAT)
    def _():
        def body(c, carry):
            scores(n - 1, c)
            for j in range(n - 1):
                update(j, c)
                scores(j, c + 1)
            update(n - 1, c)
            return carry

        lax.fori_loop(0, NT_LAT, body, 0, unroll=4)

    scores(n - 1, NT_LAT, keys=C)
    first_chunk_of_next = jnp.where(tile + 1 == NT_LAT, NT_LAT, 0)
    for j in range(n):
        update(j, NT_LAT, keys=C)
        if j < n - 1:
            scores(j, first_chunk_of_next, qk_next_tile)


def _next_tile_spec(block, n_q_tiles):
    return pl.BlockSpec(block, lambda b, i: (b, jnp.minimum(i + 1, n_q_tiles - 1), 0, 0))


def _mla_attn_kernel(qt_ref, qt_next_ref, k_ref, vt_ref, o_ref, s_ref, cmax_ref, m_ref, acc_ref):
    def scores_with(q_ref):
        return lambda j, c, keys: _dot(k_ref[_key_rows(c, keys), j * QK:(j + 1) * QK],
                                       q_ref[j * QK:(j + 1) * QK, :])

    _attend(HEADS, scores_with(qt_ref), scores_with(qt_next_ref),
            lambda j, c, keys: vt_ref[c, j * MLA_VROWS:(j + 1) * MLA_VROWS, 0:keys],
            s_ref, cmax_ref, m_ref, acc_ref, pl.program_id(1))
    for hd in range(HEADS):
        o = acc_ref[hd, 0:DV, :] / acc_ref[hd, DV:DV + 1, :]
        o_ref[:, hd * DV:(hd + 1) * DV] = o.T.astype(BF16)


def _mla_attention(qm, km, vmt, n_q_tiles):
    nb = qm.shape[0]
    resident = lambda shape: pl.BlockSpec((None,) + shape, lambda b, i: (b,) + (0,) * len(shape),
                                          pipeline_mode=pl.Buffered(1))
    return pl.pallas_call(
        _mla_attn_kernel,
        out_shape=jax.ShapeDtypeStruct((nb, n_q_tiles * T, HEADS * DV), BF16),
        grid=(nb, n_q_tiles),
        in_specs=[pl.BlockSpec((None, None, HEADS * QK, T), lambda b, i: (b, i, 0, 0)),
                  _next_tile_spec((None, None, HEADS * QK, T), n_q_tiles),
                  resident((ST, HEADS * QK)), resident((NT, HEADS * MLA_VROWS, T))],
        out_specs=pl.BlockSpec((None, T, HEADS * DV), lambda b, i: (b, i, 0)),
        scratch_shapes=[pltpu.VMEM((HEADS, T, T), F32),
                        pltpu.VMEM((HEADS, 1, T), F32), pltpu.VMEM((HEADS, 1, T), F32),
                        pltpu.VMEM((HEADS, MLA_VROWS, T), F32)],
        compiler_params=_params(2),
        name="mla_attention",
    )(qm, qm, km, vmt)


DIFF_MAPS = 2 * HEADS


MAPS_PER_BLOCK = LANES // DIFF_DQK


def _diff_attn_kernel(lam_init, qt_ref, qt_next_ref, k_ref, vt_ref, lqk_ref, g_ref, o_ref,
                      qm_ref, qm_next_ref, s_ref, cmax_ref, m_ref, acc_ref):
    dim = lax.broadcasted_iota(jnp.int32, (LANES, T), 0)
    for src_ref, maps_ref in ((qt_ref, qm_ref), (qt_next_ref, qm_next_ref)):
        for j in range(DIFF_MAPS):
            blk = j // MAPS_PER_BLOCK
            qt = src_ref[blk * LANES:(blk + 1) * LANES, :]
            maps_ref[j] = jnp.where(dim // DIFF_DQK == j % MAPS_PER_BLOCK, qt, jnp.zeros_like(qt))

    def scores_with(maps_ref):
        def qk(j, c, keys):
            blk = j // MAPS_PER_BLOCK
            return _dot(k_ref[_key_rows(c, keys), blk * LANES:(blk + 1) * LANES], maps_ref[j])
        return qk

    _attend(DIFF_MAPS, scores_with(qm_ref), scores_with(qm_next_ref),
            lambda j, c, keys: vt_ref[c, (j // 2) * DIFF_VROWS:(j // 2 + 1) * DIFF_VROWS, 0:keys],
            s_ref, cmax_ref, m_ref, acc_ref, pl.program_id(1))

    lqk = lqk_ref[...]
    lam = (jnp.exp(jnp.sum(lqk[0:1] * lqk[1:2], axis=-1, keepdims=True))
           - jnp.exp(jnp.sum(lqk[2:3] * lqk[3:4], axis=-1, keepdims=True)) + lam_init)
    attn = lambda j: acc_ref[j, 0:DIFF_DV, :] / acc_ref[j, DIFF_DV:DIFF_DV + 1, :]
    heads = []
    for hd in range(HEADS):
        d = attn(2 * hd) - lam * attn(2 * hd + 1)
        ms = jnp.mean(d * d, axis=0, keepdims=True)
        heads.append(d * lax.rsqrt(ms + EPS))
    out = jnp.concatenate(heads, axis=0).T
    o_ref[...] = ((out * g_ref[...]) * (1.0 - lam_init)).astype(BF16)


def _diff_attention(l, qd, kd, vdt, lqk, g_row, lam_init, n_q_tiles):
    nb = qd.shape[0]
    return pl.pallas_call(
        functools.partial(_diff_attn_kernel, lam_init),
        out_shape=jax.ShapeDtypeStruct((nb, n_q_tiles * T, DIFF_W), BF16),
        grid=(nb, n_q_tiles),
        in_specs=[pl.BlockSpec((None, None, DIFF_W, T), lambda b, i: (b, i, 0, 0)),
                  _next_tile_spec((None, None, DIFF_W, T), n_q_tiles),
                  pl.BlockSpec((None, ST, DIFF_W), lambda b, i: (b, 0, 0)),
                  pl.BlockSpec((None, NT, HEADS * DIFF_VROWS, T), lambda b, i: (b, 0, 0, 0)),
                  _layer_spec(l, (4, DIFF_DQK)), _layer_spec(l, (1, DIFF_W))],
        out_specs=pl.BlockSpec((None, T, DIFF_W), lambda b, i: (b, i, 0)),
        scratch_shapes=[pltpu.VMEM((DIFF_MAPS, LANES, T), BF16), pltpu.VMEM((DIFF_MAPS, LANES, T), BF16),
                        pltpu.VMEM((DIFF_MAPS, T, T), F32),
                        pltpu.VMEM((DIFF_MAPS, 1, T), F32), pltpu.VMEM((DIFF_MAPS, 1, T), F32),
                        pltpu.VMEM((DIFF_MAPS, DIFF_VROWS, T), F32)],
        compiler_params=_params(2),
        name="diff_attention",
    )(qd, qd, kd, vdt, lqk, g_row)


def _outproj_kernel(x_ref, c_ref, om_ref, od_ref, cm_ref, mod_ref, wo_ref, g_ref, b_ref, o_ref):
    r = _mod_row(pl.program_id(1))
    g1 = mod_ref[pl.ds(r, 1), 2 * D:3 * D]
    for half in range(T // C):
        rows = pl.ds(half * C, C)
        y = (_dot(om_ref[rows, :], wo_ref[0:MLA_W, :]) + _dot(od_ref[rows, :], wo_ref[MLA_W:MLA_W + DIFF_W, :])
             + _dot(cm_ref[rows, :], wo_ref[MLA_W + DIFF_W:D, :]))
        x = _stream_half(x_ref, c_ref, pl.program_id(1), half)
        o_ref[rows, :] = _ln(DN_ALPHA * x + g1 * y, g_ref[...], b_ref[...])


def _outproj(l, x, c, om, od, cm, mod, w, n_tiles):
    nb = x.shape[0]
    row = lambda width: pl.BlockSpec((None, T, width), lambda b, i: (b, i, 0))
    return pl.pallas_call(
        _outproj_kernel,
        out_shape=jax.ShapeDtypeStruct((nb, n_tiles * T, D), F32),
        grid=(nb, n_tiles),
        in_specs=_stream_specs(x, c) + [
                  row(MLA_W), row(DIFF_W), row(CHUNK_W), _layer_spec(l, (MOD_ROWS, 6 * D)),
                  _layer_spec(l, (D, D)), _layer_spec(l, (1, D)), _layer_spec(l, (1, D))],
        out_specs=row(D),
        compiler_params=_params(2),
        name="outproj",
    )(x, c, om, od, cm, mod, w["wo"], w["ln1g"], w["ln1b"])


def _ffn_kernel(x_ref, xp_ref, xn_ref, mod_ref, wup_ref, cw_ref, cb_ref, wdn_ref, g_ref, b_ref,
                o_ref, h_ref, act_ref):
    i = pl.program_id(1)
    r = _mod_row(i)
    sh2 = mod_ref[pl.ds(r, 1), 3 * D:4 * D]
    sc2 = mod_ref[pl.ds(r, 1), 4 * D:5 * D]
    g2 = mod_ref[pl.ds(r, 1), 5 * D:6 * D]
    x = x_ref[...]
    modulate = lambda t: t * (1.0 + sc2) + sh2
    seq_start = jnp.logical_or(i == 0, i == NT_LAT)
    seq_end = i >= NT_LAT - 1
    h_ref[0:HALO, :] = jnp.where(seq_start, 0.0, modulate(xp_ref[...])).astype(BF16)
    hm = modulate(x)
    row = lax.broadcasted_iota(jnp.int32, (T, 1), 0)
    hm = jnp.where(jnp.logical_and(i == NT_LAT, row >= C), 0.0, hm)
    h_ref[HALO:HALO + T, :] = hm.astype(BF16)
    h_ref[HALO + T:, :] = jnp.where(seq_end, 0.0, modulate(xn_ref[...])).astype(BF16)
    h = h_ref[...]

    def conv(col):
        up = _dot(h, wup_ref[:, col:col + FF_CHUNK])
        cw = cw_ref[:, col:col + FF_CHUNK]
        prev = pltpu.roll(up, 1, 0)[HALO:HALO + T]
        nxt = pltpu.roll(up, T + 2 * HALO - 1, 0)[HALO:HALO + T]
        return prev * cw[0:1] + up[HALO:HALO + T] * cw[1:2] + nxt * cw[2:3] + cb_ref[:, col:col + FF_CHUNK]

    for j in range(D_FF // FF_CHUNK):
        gate = conv(j * FF_CHUNK)
        val = conv(D_FF + j * FF_CHUNK)
        act_ref[:, j * FF_CHUNK:(j + 1) * FF_CHUNK] = ((gate * (1.0 / (1.0 + jnp.exp(-gate)))) * val).astype(BF16)
    for part in range(LN_PARTS):
        rows = pl.ds(part * (T // LN_PARTS), T // LN_PARTS)
        y = _dot(act_ref[rows, :], wdn_ref[...])
        o_ref[rows, :] = _ln(DN_ALPHA * x_ref[rows, :] + g2 * y, g_ref[...], b_ref[...])


def _ffn(l, x1, mod, w, n_tiles):
    nb, rows, _ = x1.shape
    nh = rows // HALO
    per = T // HALO
    return pl.pallas_call(
        _ffn_kernel,
        out_shape=jax.ShapeDtypeStruct((nb, n_tiles * T, D), F32),
        grid=(nb, n_tiles),
        in_specs=[pl.BlockSpec((None, T, D), lambda b, i: (b, i, 0)),
                  pl.BlockSpec((None, HALO, D), lambda b, i: (b, jnp.maximum(i * per - 1, 0), 0)),
                  pl.BlockSpec((None, HALO, D), lambda b, i: (b, jnp.minimum((i + 1) * per, nh - 1), 0)),
                  _layer_spec(l, (MOD_ROWS, 6 * D)), _layer_spec(l, (D, 2 * D_FF)),
                  _layer_spec(l, (3, 2 * D_FF)), _layer_spec(l, (1, 2 * D_FF)), _layer_spec(l, (D_FF, D)),
                  _layer_spec(l, (1, D)), _layer_spec(l, (1, D))],
        out_specs=pl.BlockSpec((None, T, D), lambda b, i: (b, i, 0)),
        scratch_shapes=[pltpu.VMEM((T + 2 * HALO, D), BF16), pltpu.VMEM((T, D_FF), BF16)],
        compiler_params=_params(2),
        name="conv_ffn",
    )(x1, x1, x1, mod, w["wup"], w["convw"], w["convb"], w["wdn"], w["ln2g"], w["ln2b"])


def _rope_tables():
    grid_h = S // GRID_W

    def tables(n):
        inv = ROPE_BASE ** (-np.arange(n, dtype=np.float64) / n)
        ar = np.repeat(np.arange(grid_h, dtype=np.float64)[:, None] * inv[None, :], GRID_W, axis=0)
        ac = np.tile(np.arange(GRID_W, dtype=np.float64)[:, None] * inv[None, :], (grid_h, 1))
        cos = np.concatenate([np.cos(ar), np.cos(ar), np.cos(ac), np.cos(ac)], -1)
        sin = np.concatenate([-np.sin(ar), np.sin(ar), -np.sin(ac), np.sin(ac)], -1)
        return jnp.asarray(cos, F32), jnp.asarray(sin, F32)

    return tables(ROPE // 4) + tables(DIFF_DQK // 4)


def _prepare_weights(tabs, w_in, mla_gq, mla_wuq, mla_gkv, mla_wukv, diff_lq1, diff_lk1, diff_lq2, diff_lk2,
                     diff_subln_g, sgu_ln_g, sgu_ln_b, sgu_ws, sgu_bs, w_o, ln1_g, ln1_b, ffn_wup, ffn_convw,
                     ffn_convb, ffn_wdown, ln2_g, ln2_b):
    w_in = w_in.astype(BF16)
    split = Q_LORA + KV_LORA + ROPE
    zpad = jnp.zeros((DEPTH, D, LANES - ROPE), BF16)
    win_wide = jnp.concatenate([w_in[:, :, :split], zpad, w_in[:, :, split:]], axis=2)

    wuq = mla_wuq.astype(BF16).reshape(DEPTH, Q_LORA, HEADS, NOPE + ROPE)
    zq = jnp.zeros((DEPTH, Q_LORA, HEADS, LANES - ROPE), BF16)
    wuq_main = jnp.concatenate([wuq, zq], -1).reshape(DEPTH, Q_LORA, HEADS * QK)
    ckr, skr, cd, sd = tabs
    row = lambda v: v[:, None, :]
    bias = jnp.repeat(jnp.swapaxes(sgu_bs, 1, 2), CHUNK_W // GROUPS, axis=2)
    return dict(
        win=win_wide, gq=row(mla_gq), wuq=wuq_main, gkv=row(mla_gkv),
        wkv=mla_wukv.astype(BF16),
        ckr=ckr, skr=skr, cd=cd, sd=sd, lng=row(sgu_ln_g), lnb=row(sgu_ln_b),
        ws=sgu_ws.reshape(DEPTH, GROUPS * CHUNK, CHUNK).astype(BF16), bs=bias,
        lqk=jnp.stack([diff_lq1, diff_lk1, diff_lq2, diff_lk2], axis=1),
        subg=row(jnp.tile(diff_subln_g, (1, HEADS))),
        wo=w_o.astype(BF16), ln1g=row(ln1_g), ln1b=row(ln1_b),
        wup=ffn_wup.astype(BF16), convw=ffn_convw, convb=row(ffn_convb),
        wdn=ffn_wdown.astype(BF16), ln2g=row(ln2_g), ln2b=row(ln2_b))


def kernel(x, c, ctx, c_ctx, ada_w, ada_b, w_in, mla_gq, mla_wuq, mla_gkv, mla_wukv, diff_lq1, diff_lk1, diff_lq2, diff_lk2, diff_subln_g, sgu_ln_g, sgu_ln_b, sgu_ws, sgu_bs, w_o, ln1_g, ln1_b, ffn_wup, ffn_convw, ffn_convb, ffn_wdown, ln2_g, ln2_b):
    nb = x.shape[0]
    cvec = jnp.concatenate([c, c_ctx[None], jnp.zeros((MOD_ROWS - nb - 1, D), F32)], 0)
    mod = _modulation(cvec, ada_w, ada_b)
    tabs = _rope_tables()
    w = _prepare_weights(tabs, w_in, mla_gq, mla_wuq, mla_gkv, mla_wukv, diff_lq1, diff_lk1, diff_lq2, diff_lk2,
                         diff_subln_g, sgu_ln_g, sgu_ln_b, sgu_ws, sgu_bs, w_o, ln1_g, ln1_b, ffn_wup, ffn_convw,
                         ffn_convb, ffn_wdown, ln2_g, ln2_b)
    xs, xc = x, ctx
    for l in range(DEPTH):
        n_tiles = NT if l < DEPTH - 1 else NT_LAT
        lam_init = 0.8 - 0.6 * math.exp(-0.3 * l)
        qm, km, vmt, qd, kd, vdt, cm = _inproj(l, xs, xc, mod, w)
        om = _mla_attention(qm, km, vmt, n_tiles)
        od = _diff_attention(l, qd, kd, vdt, w["lqk"], w["subg"], lam_init, n_tiles)
        x1 = _outproj(l, xs, xc, om, od, cm, mod, w, n_tiles)
        xs = xc = _ffn(l, x1, mod, w, n_tiles)
    return xs
```

```python
import functools
import math

import jax
import jax.numpy as jnp
import numpy as np
from jax import lax
from jax.experimental import pallas as pl
from jax.experimental.pallas import tpu as pltpu

F32 = jnp.float32
BF16 = jnp.bfloat16

D = 1024
S = 8192
C = 256
GRID_W = 64
DEPTH = 2
T = 512
ST = S + 2 * C
NT = ST // T
NT_LAT = S // T
HEADS = 4
Q_LORA = 384
KV_LORA = 256
NOPE = 128
ROPE = 64
DV = 128
DIFF_W = 256
DIFF_DQK = 32
DIFF_DV = 64
CHUNK_W = 256
CHUNK = 128
GROUPS = 4
D_FF = 2816
FF_CHUNK = 256
HALO = 16
LN_PARTS = 2
ONES_ROWS = 16
MLA_VROWS = DV + ONES_ROWS
DIFF_VROWS = DIFF_DV + ONES_ROWS
EPS = 1e-6
ROPE_BASE = 10000.0
DN_ALPHA = (2 * DEPTH) ** 0.25
MLA_SCALE = (NOPE + ROPE) ** -0.5
DIFF_SCALE = DIFF_DQK ** -0.5
LOG2E = math.log2(math.e)
MLA_QSCALE = MLA_SCALE * LOG2E
DIFF_QSCALE = DIFF_SCALE * LOG2E
LANES = 128
QK = NOPE + LANES
MLA_W = HEADS * DV
Z_KV = Q_LORA
Z_KR = Z_KV + KV_LORA
Z_DQ = Z_KR + LANES
Z_DK = Z_DQ + DIFF_W
Z_DV = Z_DK + DIFF_W
Z_CH = Z_DV + DIFF_W
ZW = Z_CH + 2 * CHUNK_W
MOD_ROWS = 8
CTX_ROW = 2
VMEM_LIMIT = 56 * 1024 * 1024

_NT_DIMS = (((1,), (1,)), ((), ()))


def _dot(a, b):
    return jnp.dot(a, b, preferred_element_type=F32)


def _rms(x, g):
    return (x * lax.rsqrt(jnp.mean(x * x, axis=-1, keepdims=True) + EPS)) * g


def _ln(x, g, b):
    mu = jnp.mean(x, axis=-1, keepdims=True)
    xc = x - mu
    var = jnp.mean(xc * xc, axis=-1, keepdims=True)
    return (xc * lax.rsqrt(var + EPS)) * g + b


def _params(n_axes):
    return pltpu.CompilerParams(dimension_semantics=("arbitrary",) * n_axes,
                                vmem_limit_bytes=VMEM_LIMIT)


def _layer_spec(l, shape):
    nd = len(shape)
    return pl.BlockSpec((None,) + shape, lambda *_: (l,) + (0,) * nd, pipeline_mode=pl.Buffered(1))


MOD_BLOCK = 1536


def _mod_kernel(c_ref, w_ref, b_ref, o_ref):
    c = c_ref[...]
    sc = (c * (1.0 / (1.0 + jnp.exp(-c)))).astype(BF16)
    o_ref[...] = _dot(sc, w_ref[...].astype(BF16)) + b_ref[...]


def _modulation(cvec, ada_w, ada_b):
    return pl.pallas_call(
        _mod_kernel,
        out_shape=jax.ShapeDtypeStruct((DEPTH, MOD_ROWS, 6 * D), F32),
        grid=(DEPTH, 6 * D // MOD_BLOCK),
        in_specs=[pl.BlockSpec((MOD_ROWS, D), lambda l, j: (0, 0)),
                  pl.BlockSpec((None, D, MOD_BLOCK), lambda l, j: (l, 0, j)),
                  pl.BlockSpec((None, 1, MOD_BLOCK), lambda l, j: (l, 0, j))],
        out_specs=pl.BlockSpec((None, MOD_ROWS, MOD_BLOCK), lambda l, j: (l, 0, j)),
        compiler_params=_params(2),
        name="modulation",
    )(cvec, ada_w, ada_b.reshape(DEPTH, 1, 6 * D))


def _stream_tile(x_ref, c_ref, tile):
    return jnp.concatenate([_stream_half(x_ref, c_ref, tile, half) for half in range(T // C)], axis=0)


def _stream_half(x_ref, c_ref, tile, half):
    context = c_ref[...] if half == 0 else jnp.broadcast_to(c_ref[C - 1:C, :], (C, D))
    return jnp.where(tile == NT_LAT, context, x_ref[half * C:(half + 1) * C, :])


def _stream_specs(x, c):
    ctx_block = 0 if c.shape[1] == C else S // C
    return [pl.BlockSpec((None, T, D), lambda b, i: (b, jnp.minimum(i, NT_LAT - 1), 0)),
            pl.BlockSpec((None, C, D), lambda b, i: (b, ctx_block, 0))]


def _mod_row(tile):
    return jnp.where(tile == NT_LAT, CTX_ROW, pl.program_id(0))


def _swap_lane_halves(x, half):
    lane = lax.broadcasted_iota(jnp.int32, x.shape, 1)
    return jnp.where(lane % (2 * half) < half, pltpu.roll(x, LANES - half, 1), pltpu.roll(x, half, 1))


def _rotate(x, cos, sin_signed, half):
    return x * cos + _swap_lane_halves(x, half) * sin_signed


def _inproj_kernel(x_ref, c_ref, mod_ref, win_ref, gq_ref, wuq_ref, gkv_ref, wkv_ref,
                   ckr_ref, skr_ref, cd_ref, sd_ref, lng_ref, lnb_ref, ws_ref, bs_ref,
                   qmt_ref, km_ref, vmt_ref, qdt_ref, kd_ref, vdt_ref, cm_ref):
    r = _mod_row(pl.program_id(1))
    sh1 = mod_ref[pl.ds(r, 1), 0:D]
    sc1 = mod_ref[pl.ds(r, 1), D:2 * D]
    group = lax.broadcasted_iota(jnp.int32, (CHUNK, CHUNK_W), 1) // (CHUNK_W // GROUPS)
    ones = jnp.ones((ONES_ROWS, T), BF16)
    h = (_stream_tile(x_ref, c_ref, pl.program_id(1)) * (1.0 + sc1) + sh1).astype(BF16)
    z = _dot(h, win_ref[:, 0:Z_DQ])
    positioned = pl.program_id(1) < NT_LAT
    kr_pad = jnp.zeros((T, LANES - ROPE), F32)
    ckr = jnp.concatenate([jnp.where(positioned, ckr_ref[...], 1.0), kr_pad], axis=1)
    skr = jnp.concatenate([jnp.where(positioned, skr_ref[...], 0.0), kr_pad], axis=1)
    cd = jnp.tile(jnp.where(positioned, cd_ref[...], 1.0), (1, MAPS_PER_BLOCK))
    sd = jnp.tile(jnp.where(positioned, sd_ref[...], 0.0), (1, MAPS_PER_BLOCK))

    qn = _rms(z[:, 0:Q_LORA], gq_ref[...]).astype(BF16)
    qa = _dot(qn, wuq_ref[...])
    for hd in range(HEADS):
        nope = qa[:, hd * QK:hd * QK + NOPE]
        rot = _rotate(qa[:, hd * QK + NOPE:(hd + 1) * QK], ckr, skr, ROPE // 4)
        qmt_ref[hd * QK:hd * QK + NOPE, :] = (nope * MLA_QSCALE).T.astype(BF16)
        qmt_ref[hd * QK + NOPE:(hd + 1) * QK, :] = (rot * MLA_QSCALE).T.astype(BF16)

    kvn = _rms(z[:, Q_LORA:Q_LORA + KV_LORA], gkv_ref[...]).astype(BF16)
    kv = _dot(kvn, wkv_ref[...])
    kr = _rotate(z[:, Z_KR:Z_DQ], ckr, skr, ROPE // 4).astype(BF16)
    for hd in range(HEADS):
        lo = hd * (NOPE + DV)
        km_ref[:, hd * QK:hd * QK + NOPE] = kv[:, lo:lo + NOPE].astype(BF16)
        km_ref[:, hd * QK + NOPE:(hd + 1) * QK] = kr
        vmt_ref[hd * MLA_VROWS:hd * MLA_VROWS + DV, :] = kv[:, lo + NOPE:lo + NOPE + DV].T.astype(BF16)
        vmt_ref[hd * MLA_VROWS + DV:(hd + 1) * MLA_VROWS, :] = ones

    zd = _dot(h, win_ref[:, Z_DQ:Z_CH])
    for lo in range(0, DIFF_W, LANES):
        qdt_ref[lo:lo + LANES, :] = (_rotate(zd[:, lo:lo + LANES], cd, sd, DIFF_DQK // 4)
                                     * DIFF_QSCALE).T.astype(BF16)
        kd_ref[:, lo:lo + LANES] = _rotate(zd[:, DIFF_W + lo:DIFF_W + lo + LANES], cd, sd,
                                           DIFF_DQK // 4).astype(BF16)
    vt = zd[:, 2 * DIFF_W:3 * DIFF_W].T.astype(BF16)
    for hd in range(HEADS):
        vdt_ref[hd * DIFF_VROWS:hd * DIFF_VROWS + DIFF_DV, :] = vt[hd * DIFF_DV:(hd + 1) * DIFF_DV]
        vdt_ref[hd * DIFF_VROWS + DIFF_DV:(hd + 1) * DIFF_VROWS, :] = ones

    zc = _dot(h, win_ref[:, Z_CH:ZW])
    zc = 0.5 * zc * (1.0 + jnp.tanh(math.sqrt(2.0 / math.pi) * (zc + 0.044715 * (zc * zc * zc))))
    u = zc[:, :CHUNK_W]
    v = _ln(zc[:, CHUNK_W:], lng_ref[...], lnb_ref[...]).astype(BF16)
    for n in range(T // CHUNK):
        mixed_all = _dot(ws_ref[...], v[n * CHUNK:(n + 1) * CHUNK, :])
        mixed = mixed_all[0:CHUNK]
        for g in range(1, GROUPS):
            mixed = jnp.where(group == g, mixed_all[g * CHUNK:(g + 1) * CHUNK], mixed)
        cm_ref[n * CHUNK:(n + 1) * CHUNK, :] = (u[n * CHUNK:(n + 1) * CHUNK, :]
                                                * (mixed + bs_ref[...])).astype(BF16)


def _inproj(l, x, c, mod, w):
    nb = x.shape[0]
    row = lambda width: pl.BlockSpec((None, T, width), lambda b, i: (b, i, 0))
    tab = lambda width: pl.BlockSpec((T, width), lambda b, i: (jnp.minimum(i, NT_LAT - 1), 0))
    rows = lambda width: jax.ShapeDtypeStruct((nb, ST, width), BF16)
    cols = lambda width: jax.ShapeDtypeStruct((nb, NT, width, T), BF16)
    col = lambda width: pl.BlockSpec((None, None, width, T), lambda b, i: (b, i, 0, 0))
    outs = [cols(HEADS * QK), rows(HEADS * QK), cols(HEADS * MLA_VROWS), cols(DIFF_W), rows(DIFF_W),
            cols(HEADS * DIFF_VROWS), rows(CHUNK_W)]
    return pl.pallas_call(
        _inproj_kernel,
        out_shape=outs,
        grid=(nb, NT),
        in_specs=_stream_specs(x, c) + [
                  _layer_spec(l, (MOD_ROWS, 6 * D)), _layer_spec(l, (D, ZW)),
                  _layer_spec(l, (1, Q_LORA)), _layer_spec(l, (Q_LORA, HEADS * QK)),
                  _layer_spec(l, (1, KV_LORA)), _layer_spec(l, (KV_LORA, HEADS * (NOPE + DV))),
                  tab(ROPE), tab(ROPE), tab(DIFF_DQK), tab(DIFF_DQK),
                  _layer_spec(l, (1, CHUNK_W)), _layer_spec(l, (1, CHUNK_W)),
                  _layer_spec(l, (GROUPS * CHUNK, CHUNK)), _layer_spec(l, (CHUNK, CHUNK_W))],
        out_specs=[col(HEADS * QK), row(HEADS * QK), col(HEADS * MLA_VROWS), col(DIFF_W), row(DIFF_W),
                   col(HEADS * DIFF_VROWS), row(CHUNK_W)],
        compiler_params=_params(2),
        name="inproj",
    )(x, c, mod, w["win"], w["gq"], w["wuq"], w["gkv"], w["wkv"],
      w["ckr"], w["skr"], w["cd"], w["sd"], w["lng"], w["lnb"], w["ws"], w["bs"])


def _key_rows(c, keys=T):
    if isinstance(c, int):
        return pl.ds(c * T, keys)
    return pl.ds(pl.multiple_of(c * T, T), keys)


def _attend(n, qk, qk_next_tile, vt, s_ref, cmax_ref, m_ref, acc_ref, tile, unroll):
    m_ref[...] = jnp.full(m_ref.shape, -jnp.inf, F32)
    acc_ref[...] = jnp.zeros(acc_ref.shape, F32)

    def scores(j, c, qk=qk, keys=T):
        s = qk(j, c, keys)
        s_ref[j, 0:keys, :] = s
        cmax_ref[j] = jnp.max(s, axis=0, keepdims=True)

    def update(j, c, keys=T):
        m_prev = m_ref[j]
        m_new = jnp.maximum(m_prev, cmax_ref[j])
        alpha = jnp.exp2(m_prev - m_new)
        p = jnp.exp2(s_ref[j, 0:keys, :] - m_new).astype(BF16)
        acc_ref[j] = alpha * acc_ref[j] + _dot(vt(j, c, keys), p)
        m_ref[j] = m_new

    @pl.when(tile == 0)
    def _():
        for j in range(n - 1):
            scores(j, 0)

    @pl.when(tile < NT_LAT)
    def _():
        def body(c, carry):
            scores(n - 1, c)
            for j in range(n - 1):
                update(j, c)
                scores(j, c + 1)
            update(n - 1, c)
            return carry

        lax.fori_loop(0, NT_LAT, body, 0, unroll=unroll)

    scores(n - 1, NT_LAT, keys=C)
    first_chunk_of_next = jnp.where(tile + 1 == NT_LAT, NT_LAT, 0)
    for j in range(n):
        update(j, NT_LAT, keys=C)
        if j < n - 1:
            scores(j, first_chunk_of_next, qk_next_tile)


def _next_tile_spec(block, n_q_tiles):
    return pl.BlockSpec(block, lambda b, i: (b, jnp.minimum(i + 1, n_q_tiles - 1), 0, 0))


def _mla_attn_kernel(qt_ref, qt_next_ref, k_ref, vt_ref, o_ref, s_ref, cmax_ref, m_ref, acc_ref):
    def scores_with(q_ref):
        return lambda j, c, keys: _dot(k_ref[_key_rows(c, keys), j * QK:(j + 1) * QK],
                                       q_ref[j * QK:(j + 1) * QK, :])

    _attend(HEADS, scores_with(qt_ref), scores_with(qt_next_ref),
            lambda j, c, keys: vt_ref[c, j * MLA_VROWS:(j + 1) * MLA_VROWS, 0:keys],
            s_ref, cmax_ref, m_ref, acc_ref, pl.program_id(1), unroll=8)
    for hd in range(HEADS):
        o = acc_ref[hd, 0:DV, :] / acc_ref[hd, DV:DV + 1, :]
        o_ref[:, hd * DV:(hd + 1) * DV] = o.T.astype(BF16)


def _mla_attention(qm, km, vmt, n_q_tiles):
    nb = qm.shape[0]
    resident = lambda shape: pl.BlockSpec((None,) + shape, lambda b, i: (b,) + (0,) * len(shape),
                                          pipeline_mode=pl.Buffered(1))
    return pl.pallas_call(
        _mla_attn_kernel,
        out_shape=jax.ShapeDtypeStruct((nb, n_q_tiles * T, HEADS * DV), BF16),
        grid=(nb, n_q_tiles),
        in_specs=[pl.BlockSpec((None, None, HEADS * QK, T), lambda b, i: (b, i, 0, 0)),
                  _next_tile_spec((None, None, HEADS * QK, T), n_q_tiles),
                  resident((ST, HEADS * QK)), resident((NT, HEADS * MLA_VROWS, T))],
        out_specs=pl.BlockSpec((None, T, HEADS * DV), lambda b, i: (b, i, 0)),
        scratch_shapes=[pltpu.VMEM((HEADS, T, T), F32),
                        pltpu.VMEM((HEADS, 1, T), F32), pltpu.VMEM((HEADS, 1, T), F32),
                        pltpu.VMEM((HEADS, MLA_VROWS, T), F32)],
        compiler_params=_params(2),
        name="mla_attention",
    )(qm, qm, km, vmt)


DIFF_MAPS = 2 * HEADS


MAPS_PER_BLOCK = LANES // DIFF_DQK


def _diff_attn_kernel(lam_init, qt_ref, qt_next_ref, k_ref, vt_ref, lqk_ref, g_ref, o_ref,
                      qm_ref, qm_next_ref, s_ref, cmax_ref, m_ref, acc_ref):
    dim = lax.broadcasted_iota(jnp.int32, (LANES, T), 0)
    for src_ref, maps_ref in ((qt_ref, qm_ref), (qt_next_ref, qm_next_ref)):
        for j in range(DIFF_MAPS):
            blk = j // MAPS_PER_BLOCK
            qt = src_ref[blk * LANES:(blk + 1) * LANES, :]
            maps_ref[j] = jnp.where(dim // DIFF_DQK == j % MAPS_PER_BLOCK, qt, jnp.zeros_like(qt))

    def scores_with(maps_ref):
        def qk(j, c, keys):
            blk = j // MAPS_PER_BLOCK
            return _dot(k_ref[_key_rows(c, keys), blk * LANES:(blk + 1) * LANES], maps_ref[j])
        return qk

    _attend(DIFF_MAPS, scores_with(qm_ref), scores_with(qm_next_ref),
            lambda j, c, keys: vt_ref[c, (j // 2) * DIFF_VROWS:(j // 2 + 1) * DIFF_VROWS, 0:keys],
            s_ref, cmax_ref, m_ref, acc_ref, pl.program_id(1), unroll=4)

    lqk = lqk_ref[...]
    lam = (jnp.exp(jnp.sum(lqk[0:1] * lqk[1:2], axis=-1, keepdims=True))
           - jnp.exp(jnp.sum(lqk[2:3] * lqk[3:4], axis=-1, keepdims=True)) + lam_init)
    attn = lambda j: acc_ref[j, 0:DIFF_DV, :] / acc_ref[j, DIFF_DV:DIFF_DV + 1, :]
    heads = []
    for hd in range(HEADS):
        d = attn(2 * hd) - lam * attn(2 * hd + 1)
        ms = jnp.mean(d * d, axis=0, keepdims=True)
        heads.append(d * lax.rsqrt(ms + EPS))
    out = jnp.concatenate(heads, axis=0).T
    o_ref[...] = ((out * g_ref[...]) * (1.0 - lam_init)).astype(BF16)


def _diff_attention(l, qd, kd, vdt, lqk, g_row, lam_init, n_q_tiles):
    nb = qd.shape[0]
    return pl.pallas_call(
        functools.partial(_diff_attn_kernel, lam_init),
        out_shape=jax.ShapeDtypeStruct((nb, n_q_tiles * T, DIFF_W), BF16),
        grid=(nb, n_q_tiles),
        in_specs=[pl.BlockSpec((None, None, DIFF_W, T), lambda b, i: (b, i, 0, 0)),
                  _next_tile_spec((None, None, DIFF_W, T), n_q_tiles),
                  pl.BlockSpec((None, ST, DIFF_W), lambda b, i: (b, 0, 0)),
                  pl.BlockSpec((None, NT, HEADS * DIFF_VROWS, T), lambda b, i: (b, 0, 0, 0)),
                  _layer_spec(l, (4, DIFF_DQK)), _layer_spec(l, (1, DIFF_W))],
        out_specs=pl.BlockSpec((None, T, DIFF_W), lambda b, i: (b, i, 0)),
        scratch_shapes=[pltpu.VMEM((DIFF_MAPS, LANES, T), BF16), pltpu.VMEM((DIFF_MAPS, LANES, T), BF16),
                        pltpu.VMEM((DIFF_MAPS, T, T), F32),
                        pltpu.VMEM((DIFF_MAPS, 1, T), F32), pltpu.VMEM((DIFF_MAPS, 1, T), F32),
                        pltpu.VMEM((DIFF_MAPS, DIFF_VROWS, T), F32)],
        compiler_params=_params(2),
        name="diff_attention",
    )(qd, qd, kd, vdt, lqk, g_row)


def _outproj_kernel(x_ref, c_ref, om_ref, od_ref, cm_ref, mod_ref, wo_ref, g_ref, b_ref, o_ref):
    r = _mod_row(pl.program_id(1))
    g1 = mod_ref[pl.ds(r, 1), 2 * D:3 * D]
    for half in range(T // C):
        rows = pl.ds(half * C, C)
        y = (_dot(om_ref[rows, :], wo_ref[0:MLA_W, :]) + _dot(od_ref[rows, :], wo_ref[MLA_W:MLA_W + DIFF_W, :])
             + _dot(cm_ref[rows, :], wo_ref[MLA_W + DIFF_W:D, :]))
        x = _stream_half(x_ref, c_ref, pl.program_id(1), half)
        o_ref[rows, :] = _ln(DN_ALPHA * x + g1 * y, g_ref[...], b_ref[...])


def _outproj(l, x, c, om, od, cm, mod, w, n_tiles):
    nb = x.shape[0]
    row = lambda width: pl.BlockSpec((None, T, width), lambda b, i: (b, i, 0))
    return pl.pallas_call(
        _outproj_kernel,
        out_shape=jax.ShapeDtypeStruct((nb, n_tiles * T, D), F32),
        grid=(nb, n_tiles),
        in_specs=_stream_specs(x, c) + [
                  row(MLA_W), row(DIFF_W), row(CHUNK_W), _layer_spec(l, (MOD_ROWS, 6 * D)),
                  _layer_spec(l, (D, D)), _layer_spec(l, (1, D)), _layer_spec(l, (1, D))],
        out_specs=row(D),
        compiler_params=_params(2),
        name="outproj",
    )(x, c, om, od, cm, mod, w["wo"], w["ln1g"], w["ln1b"])


def _ffn_kernel(x_ref, xp_ref, xn_ref, mod_ref, wup_ref, cw_ref, cb_ref, wdn_ref, g_ref, b_ref,
                o_ref, h_ref, act_ref):
    i = pl.program_id(1)
    r = _mod_row(i)
    sh2 = mod_ref[pl.ds(r, 1), 3 * D:4 * D]
    sc2 = mod_ref[pl.ds(r, 1), 4 * D:5 * D]
    g2 = mod_ref[pl.ds(r, 1), 5 * D:6 * D]
    x = x_ref[...]
    modulate = lambda t: t * (1.0 + sc2) + sh2
    seq_start = jnp.logical_or(i == 0, i == NT_LAT)
    seq_end = i >= NT_LAT - 1
    h_ref[0:HALO, :] = jnp.where(seq_start, 0.0, modulate(xp_ref[...])).astype(BF16)
    hm = modulate(x)
    row = lax.broadcasted_iota(jnp.int32, (T, 1), 0)
    hm = jnp.where(jnp.logical_and(i == NT_LAT, row >= C), 0.0, hm)
    h_ref[HALO:HALO + T, :] = hm.astype(BF16)
    h_ref[HALO + T:, :] = jnp.where(seq_end, 0.0, modulate(xn_ref[...])).astype(BF16)
    h = h_ref[...]

    def conv(col):
        up = _dot(h, wup_ref[:, col:col + FF_CHUNK])
        cw = cw_ref[:, col:col + FF_CHUNK]
        prev = pltpu.roll(up, 1, 0)[HALO:HALO + T]
        nxt = pltpu.roll(up, T + 2 * HALO - 1, 0)[HALO:HALO + T]
        return prev * cw[0:1] + up[HALO:HALO + T] * cw[1:2] + nxt * cw[2:3] + cb_ref[:, col:col + FF_CHUNK]

    for j in range(D_FF // FF_CHUNK):
        gate = conv(j * FF_CHUNK)
        val = conv(D_FF + j * FF_CHUNK)
        act_ref[:, j * FF_CHUNK:(j + 1) * FF_CHUNK] = ((gate * (1.0 / (1.0 + jnp.exp(-gate)))) * val).astype(BF16)
    for part in range(LN_PARTS):
        rows = pl.ds(part * (T // LN_PARTS), T // LN_PARTS)
        y = _dot(act_ref[rows, :], wdn_ref[...])
        o_ref[rows, :] = _ln(DN_ALPHA * x_ref[rows, :] + g2 * y, g_ref[...], b_ref[...])


def _ffn(l, x1, mod, w, n_tiles):
    nb, rows, _ = x1.shape
    nh = rows // HALO
    per = T // HALO
    return pl.pallas_call(
        _ffn_kernel,
        out_shape=jax.ShapeDtypeStruct((nb, n_tiles * T, D), F32),
        grid=(nb, n_tiles),
        in_specs=[pl.BlockSpec((None, T, D), lambda b, i: (b, i, 0)),
                  pl.BlockSpec((None, HALO, D), lambda b, i: (b, jnp.maximum(i * per - 1, 0), 0)),
                  pl.BlockSpec((None, HALO, D), lambda b, i: (b, jnp.minimum((i + 1) * per, nh - 1), 0)),
                  _layer_spec(l, (MOD_ROWS, 6 * D)), _layer_spec(l, (D, 2 * D_FF)),
                  _layer_spec(l, (3, 2 * D_FF)), _layer_spec(l, (1, 2 * D_FF)), _layer_spec(l, (D_FF, D)),
                  _layer_spec(l, (1, D)), _layer_spec(l, (1, D))],
        out_specs=pl.BlockSpec((None, T, D), lambda b, i: (b, i, 0)),
        scratch_shapes=[pltpu.VMEM((T + 2 * HALO, D), BF16), pltpu.VMEM((T, D_FF), BF16)],
        compiler_params=_params(2),
        name="conv_ffn",
    )(x1, x1, x1, mod, w["wup"], w["convw"], w["convb"], w["wdn"], w["ln2g"], w["ln2b"])


def _rope_tables():
    grid_h = S // GRID_W

    def tables(n):
        inv = ROPE_BASE ** (-np.arange(n, dtype=np.float64) / n)
        ar = np.repeat(np.arange(grid_h, dtype=np.float64)[:, None] * inv[None, :], GRID_W, axis=0)
        ac = np.tile(np.arange(GRID_W, dtype=np.float64)[:, None] * inv[None, :], (grid_h, 1))
        cos = np.concatenate([np.cos(ar), np.cos(ar), np.cos(ac), np.cos(ac)], -1)
        sin = np.concatenate([-np.sin(ar), np.sin(ar), -np.sin(ac), np.sin(ac)], -1)
        return jnp.asarray(cos, F32), jnp.asarray(sin, F32)

    return tables(ROPE // 4) + tables(DIFF_DQK // 4)


def _prepare_weights(tabs, w_in, mla_gq, mla_wuq, mla_gkv, mla_wukv, diff_lq1, diff_lk1, diff_lq2, diff_lk2,
                     diff_subln_g, sgu_ln_g, sgu_ln_b, sgu_ws, sgu_bs, w_o, ln1_g, ln1_b, ffn_wup, ffn_convw,
                     ffn_convb, ffn_wdown, ln2_g, ln2_b):
    w_in = w_in.astype(BF16)
    split = Q_LORA + KV_LORA + ROPE
    zpad = jnp.zeros((DEPTH, D, LANES - ROPE), BF16)
    win_wide = jnp.concatenate([w_in[:, :, :split], zpad, w_in[:, :, split:]], axis=2)

    wuq = mla_wuq.astype(BF16).reshape(DEPTH, Q_LORA, HEADS, NOPE + ROPE)
    zq = jnp.zeros((DEPTH, Q_LORA, HEADS, LANES - ROPE), BF16)
    wuq_main = jnp.concatenate([wuq, zq], -1).reshape(DEPTH, Q_LORA, HEADS * QK)
    ckr, skr, cd, sd = tabs
    row = lambda v: v[:, None, :]
    bias = jnp.repeat(jnp.swapaxes(sgu_bs, 1, 2), CHUNK_W // GROUPS, axis=2)
    return dict(
        win=win_wide, gq=row(mla_gq), wuq=wuq_main, gkv=row(mla_gkv),
        wkv=mla_wukv.astype(BF16),
        ckr=ckr, skr=skr, cd=cd, sd=sd, lng=row(sgu_ln_g), lnb=row(sgu_ln_b),
        ws=sgu_ws.reshape(DEPTH, GROUPS * CHUNK, CHUNK).astype(BF16), bs=bias,
        lqk=jnp.stack([diff_lq1, diff_lk1, diff_lq2, diff_lk2], axis=1),
        subg=row(jnp.tile(diff_subln_g, (1, HEADS))),
        wo=w_o.astype(BF16), ln1g=row(ln1_g), ln1b=row(ln1_b),
        wup=ffn_wup.astype(BF16), convw=ffn_convw, convb=row(ffn_convb),
        wdn=ffn_wdown.astype(BF16), ln2g=row(ln2_g), ln2b=row(ln2_b))


def kernel(x, c, ctx, c_ctx, ada_w, ada_b, w_in, mla_gq, mla_wuq, mla_gkv, mla_wukv, diff_lq1, diff_lk1, diff_lq2, diff_lk2, diff_subln_g, sgu_ln_g, sgu_ln_b, sgu_ws, sgu_bs, w_o, ln1_g, ln1_b, ffn_wup, ffn_convw, ffn_convb, ffn_wdown, ln2_g, ln2_b):
    nb = x.shape[0]
    cvec = jnp.concatenate([c, c_ctx[None], jnp.zeros((MOD_ROWS - nb - 1, D), F32)], 0)
    mod = _modulation(cvec, ada_w, ada_b)
    tabs = _rope_tables()
    w = _prepare_weights(tabs, w_in, mla_gq, mla_wuq, mla_gkv, mla_wukv, diff_lq1, diff_lk1, diff_lq2, diff_lk2,
                         diff_subln_g, sgu_ln_g, sgu_ln_b, sgu_ws, sgu_bs, w_o, ln1_g, ln1_b, ffn_wup, ffn_convw,
                         ffn_convb, ffn_wdown, ln2_g, ln2_b)
    xs, xc = x, ctx
    for l in range(DEPTH):
        n_tiles = NT if l < DEPTH - 1 else NT_LAT
        lam_init = 0.8 - 0.6 * math.exp(-0.3 * l)
        qm, km, vmt, qd, kd, vdt, cm = _inproj(l, xs, xc, mod, w)
        om = _mla_attention(qm, km, vmt, n_tiles)
        od = _diff_attention(l, qd, kd, vdt, w["lqk"], w["subg"], lam_init, n_tiles)
        x1 = _outproj(l, xs, xc, om, od, cm, mod, w, n_tiles)
        xs = xc = _ffn(l, x1, mod, w, n_tiles)
    return xs
```
